```python
import jax, jax.numpy as jnp
from jax import lax
import numpy as np

D_MODEL = 1024
BATCH = 2
SEQ = 16384
DEPTH = 4

PLE_DIM = 256
GM_HEADS = 8
GM_HEAD_DIM = 64
GM_WIDTH = GM_HEADS * GM_HEAD_DIM
GM_CHUNK = 128
ML_HEADS = 4
ML_HEAD_DIM = 128
ML_WIDTH = ML_HEADS * ML_HEAD_DIM
ML_CHUNK = 128
QK_CONV = 4
FFN_CONV = 3
D_FF = 2816
D_MIX = GM_WIDTH + ML_WIDTH
IN_COLS = 2 * GM_WIDTH + 4 * ML_WIDTH + 2 * ML_HEADS
EPS = 1e-6

kernel_name = "hymba_style_gmlp_mlstm_hybrid"


def rms_norm(x, g):
    xf = x.astype(jnp.float32)
    y = xf * lax.rsqrt(jnp.mean(xf * xf, axis=-1, keepdims=True) + EPS)
    return (y * g).astype(x.dtype)


def layer_norm(x, g, b):
    xf = x.astype(jnp.float32)
    mu = jnp.mean(xf, axis=-1, keepdims=True)
    xc = xf - mu
    y = xc * lax.rsqrt(jnp.mean(xc * xc, axis=-1, keepdims=True) + EPS)
    return (y * g + b).astype(x.dtype)


def causal_dwconv(x, w, b):
    K = w.shape[0]
    S = x.shape[1]
    xp = jnp.pad(x, ((0, 0), (K - 1, 0), (0, 0)))
    out = xp[:, 0:S] * w[0]
    for j in range(1, K):
        out = out + xp[:, j:j + S] * w[j]
    return out + b


def gmlp_mixer(u, v, ln_g, ln_b, ws, bs):
    Bn, S, _ = u.shape
    n = S // GM_CHUNK
    u = jax.nn.gelu(u, approximate=False)
    v = layer_norm(jax.nn.gelu(v, approximate=False), ln_g, ln_b)
    vc = v.reshape(Bn, n, GM_CHUNK, GM_HEADS, GM_HEAD_DIM)
    causal = jnp.tril(jnp.ones((GM_CHUNK, GM_CHUNK), dtype=bool))
    ws = jnp.where(causal[None], ws, jnp.zeros_like(ws))
    mixed = jnp.einsum('hts,bcshd->bcthd', ws, vc) + bs.T[:, :, None]
    return u * mixed.reshape(Bn, S, GM_WIDTH)


def _mlstm_state_step(carry, xs):
    C, nv, m = carry
    k_c, v_c, a_c, bL_c = xs
    m_new = jnp.maximum(bL_c + m, jnp.max(a_c, axis=-1))
    decay = jnp.exp(bL_c + m - m_new)
    w = jnp.exp(a_c - m_new[..., None])
    C_new = decay[..., None, None] * C + jnp.einsum('bhl,bhld,bhle->bhde', w, k_c, v_c)
    n_new = decay[..., None] * nv + jnp.einsum('bhl,bhld->bhd', w, k_c)
    return (C_new, n_new, m_new), (C, nv, m)


def mlstm_mixer(q, k, v, ig, fg, o, norm_g):
    Bn, S, _ = q.shape
    n = S // ML_CHUNK
    L, H, dh = ML_CHUNK, ML_HEADS, ML_HEAD_DIM
    f32 = jnp.float32

    def heads(t):
        return t.astype(f32).reshape(Bn, n, L, H, dh).transpose(0, 3, 1, 2, 4)

    def gates(t):
        return t.astype(f32).reshape(Bn, n, L, H).transpose(0, 3, 1, 2)

    qh = heads(q)
    kh = heads(k) * (dh ** -0.5)
    vh = heads(v)
    ig = gates(ig)
    lf = jax.nn.log_sigmoid(gates(fg))
    b = jnp.cumsum(lf, axis=-1)
    bL = b[..., -1]
    a = bL[..., None] - b + ig

    mv = lambda t: jnp.moveaxis(t, 2, 0)
    init = (jnp.zeros((Bn, H, dh, dh), f32), jnp.zeros((Bn, H, dh), f32), jnp.zeros((Bn, H), f32))
    _, (C0, n0, m0) = lax.scan(_mlstm_state_step, init, (mv(kh), mv(vh), mv(a), mv(bL)))
    C0 = jnp.moveaxis(C0, 0, 2)
    n0 = jnp.moveaxis(n0, 0, 2)
    m0 = jnp.moveaxis(m0, 0, 2)

    causal = jnp.tril(jnp.ones((L, L), dtype=bool))
    logD = jnp.where(causal, b[..., :, None] - b[..., None, :] + ig[..., None, :], -jnp.inf)
    inter = b + m0[..., None]
    m_t = jnp.maximum(jnp.max(logD, axis=-1), inter)
    wD = jnp.exp(logD - m_t[..., None])
    inter_w = jnp.exp(inter - m_t)
    s = jnp.einsum('bhntd,bhnsd->bhnts', qh, kh) * wD
    num = inter_w[..., None] * jnp.einsum('bhntd,bhnde->bhnte', qh, C0) + jnp.einsum('bhnts,bhnse->bhnte', s, vh)
    den = inter_w * jnp.einsum('bhntd,bhnd->bhnt', qh, n0) + jnp.sum(s, axis=-1)
    hr = num / jnp.maximum(jnp.abs(den), jnp.exp(-m_t))[..., None]
    hr = hr.transpose(0, 2, 3, 1, 4).reshape(Bn, S, H, dh)
    hc = jax.nn.sigmoid(o.astype(f32)).reshape(Bn, S, H, dh) * hr
    hn = hc * lax.rsqrt(jnp.mean(hc * hc, axis=-1, keepdims=True) + EPS) * norm_g.reshape(H, dh)
    return hn.reshape(Bn, S, ML_WIDTH).astype(v.dtype)


def setup_inputs(seed: int = 0) -> dict:
    key = jax.random.key(seed)
    ks = jax.random.split(key, 24)
    nrm = lambda k, shape, sc: jax.random.normal(k, shape, jnp.float32) * sc
    f_bias = jnp.broadcast_to(jnp.linspace(3.0, 6.0, ML_HEADS, dtype=jnp.float32), (DEPTH, ML_HEADS))
    return {
        "x": nrm(ks[0], (BATCH, SEQ, D_MODEL), 1.0),
        "p": nrm(ks[1], (DEPTH, BATCH, SEQ, PLE_DIM), 1.0),
        "g_mix": 1.0 + nrm(ks[2], (DEPTH, D_MODEL), 0.02),
        "w_in": nrm(ks[3], (DEPTH, D_MODEL, IN_COLS), D_MODEL ** -0.5),
        "gm_ln_g": 1.0 + nrm(ks[4], (DEPTH, GM_WIDTH), 0.02),
        "gm_ln_b": nrm(ks[5], (DEPTH, GM_WIDTH), 0.02),
        "gm_ws": nrm(ks[6], (DEPTH, GM_HEADS, GM_CHUNK, GM_CHUNK), GM_CHUNK ** -0.5),
        "gm_bs": 1.0 + nrm(ks[7], (DEPTH, GM_HEADS, GM_CHUNK), 0.02),
        "ml_conv_w": nrm(ks[8], (DEPTH, QK_CONV, 2 * ML_WIDTH), QK_CONV ** -0.5),
        "ml_conv_b": nrm(ks[9], (DEPTH, 2 * ML_WIDTH), 0.02),
        "ml_b_i": nrm(ks[10], (DEPTH, ML_HEADS), 0.1),
        "ml_b_f": f_bias + nrm(ks[11], (DEPTH, ML_HEADS), 0.1),
        "ml_norm_g": 1.0 + nrm(ks[12], (DEPTH, ML_WIDTH), 0.02),
        "w_out": nrm(ks[13], (DEPTH, D_MIX, D_MODEL), D_MIX ** -0.5),
        "g_ffn": 1.0 + nrm(ks[14], (DEPTH, D_MODEL), 0.02),
        "w_up": nrm(ks[15], (DEPTH, D_MODEL, 2 * D_FF), D_MODEL ** -0.5),
        "ffn_conv_w": nrm(ks[16], (DEPTH, FFN_CONV, 2 * D_FF), FFN_CONV ** -0.5),
        "ffn_conv_b": nrm(ks[17], (DEPTH, 2 * D_FF), 0.02),
        "w_down": nrm(ks[18], (DEPTH, D_FF, D_MODEL), D_FF ** -0.5),
        "g_ple": 1.0 + nrm(ks[19], (DEPTH, D_MODEL), 0.02),
        "w_ple_gate": nrm(ks[20], (DEPTH, D_MODEL, D_MODEL), D_MODEL ** -0.5),
        "w_ple": nrm(ks[21], (DEPTH, PLE_DIM, D_MODEL), PLE_DIM ** -0.5),
        "g_final": 1.0 + nrm(ks[22], (D_MODEL,), 0.02),
    }


def reference(x, p, g_mix, w_in, gm_ln_g, gm_ln_b, gm_ws, gm_bs, ml_conv_w, ml_conv_b, ml_b_i, ml_b_f, ml_norm_g, w_out, g_ffn, w_up, ffn_conv_w, ffn_conv_b, w_down, g_ple, w_ple_gate, w_ple, g_final):
    cuts = [GM_WIDTH, 2 * GM_WIDTH, 2 * GM_WIDTH + 2 * ML_WIDTH, 2 * GM_WIDTH + 3 * ML_WIDTH,
            2 * GM_WIDTH + 4 * ML_WIDTH, 2 * GM_WIDTH + 4 * ML_WIDTH + ML_HEADS]
    for i in range(DEPTH):
        h = rms_norm(x, g_mix[i])
        proj = h @ w_in[i]
        u, v, qk, vm, o, ig, fg = jnp.split(proj, cuts, axis=-1)
        gm = gmlp_mixer(u, v, gm_ln_g[i], gm_ln_b[i], gm_ws[i], gm_bs[i])
        qk = jax.nn.silu(causal_dwconv(qk, ml_conv_w[i], ml_conv_b[i]))
        q, k = jnp.split(qk, 2, axis=-1)
        ml = mlstm_mixer(q, k, vm, ig + ml_b_i[i], fg + ml_b_f[i], o, ml_norm_g[i])
        x = x + jnp.concatenate([gm, ml], axis=-1) @ w_out[i]
        h = rms_norm(x, g_ffn[i])
        up = causal_dwconv(h @ w_up[i], ffn_conv_w[i], ffn_conv_b[i])
        a, bv = jnp.split(up, 2, axis=-1)
        x = x + (jax.nn.silu(a) * bv) @ w_down[i]
        gate = jax.nn.sigmoid(rms_norm(x, g_ple[i]) @ w_ple_gate[i])
        x = x + gate * (p[i] @ w_ple[i])
    return rms_norm(x, g_final)
```

```python
import functools

import jax
import jax.numpy as jnp
from jax import lax
from jax.experimental import pallas as pl
from jax.experimental.pallas import tpu as pltpu

EPS = 1e-6
CHUNK = 128
GM_HEADS, GM_HEAD_DIM = 8, 64
GM_WIDTH = GM_HEADS * GM_HEAD_DIM
ML_HEADS, ML_HEAD_DIM = 4, 128
ML_WIDTH = ML_HEADS * ML_HEAD_DIM
QK_CONV, FFN_CONV = 4, 3
TAIL = 8
LANES = 128
FFN_BLOCK = 256
VMEM_LIMIT_BYTES = 56 * 1024 * 1024

F32 = jnp.float32
BF16 = jnp.bfloat16


def _dot(a, b):
    return jnp.dot(a, b, preferred_element_type=F32)


def _rms_norm(x, g):
    return x * lax.rsqrt(jnp.mean(x * x, axis=-1, keepdims=True) + EPS) * g


def _gelu(x):
    return 0.5 * x * (1.0 + lax.erf(x * (2.0 ** -0.5)))


def _split3_bf16(x):
    hi = x.astype(BF16)
    r = x - hi.astype(F32)
    mid = r.astype(BF16)
    lo = (r - mid.astype(F32)).astype(BF16)
    return hi, mid, lo


def _mix_kernel(x_ref, g_ref, win_ref, wgate_ref, gbias_ref, lng_ref, lnb_ref,
                ws_ref, bs_ref, cw_ref, cb_ref, ng_ref, wout_ref, o_ref,
                uv_s, qk_s, vo_s, gates_s, mix_s, c_s, m_s, *, tm):
    L, dh = CHUNK, ML_HEAD_DIM

    @pl.when(pl.program_id(1) == 0)
    def _():
        qk_s[0:TAIL, :] = jnp.zeros((TAIL, qk_s.shape[1]), F32)
        c_s[...] = jnp.zeros(c_s.shape, F32)
        m_s[...] = jnp.zeros(m_s.shape, F32)

    x = x_ref[0]
    h = _rms_norm(x, g_ref[...]).astype(BF16)
    uv_s[...] = _dot(h, win_ref[0])
    qk_s[TAIL:TAIL + tm, :] = _dot(h, win_ref[1])
    vo_s[...] = _dot(h, win_ref[2])
    gates_s[...] = _dot(h, wgate_ref[...]) + gbias_ref[...]

    row = lax.broadcasted_iota(jnp.int32, (L, L), 0)
    col = lax.broadcasted_iota(jnp.int32, (L, L), 1)
    causal = col <= row
    tri = causal.astype(BF16)
    lane = lax.broadcasted_iota(jnp.int32, (L, LANES), 1)
    lo_half = lane < GM_HEAD_DIM
    ones_col = (lane == 0).astype(BF16)
    causal2 = jnp.concatenate([causal, causal], axis=1)
    ws_masked = [jnp.where(causal2, ws_ref[j], 0.0).astype(BF16) for j in range(GM_HEADS // 2)]
    k_scale = dh ** -0.5

    for c in range(tm // L):
        r0 = c * L
        u = _gelu(uv_s[r0:r0 + L, 0:GM_WIDTH])
        v = _gelu(uv_s[r0:r0 + L, GM_WIDTH:2 * GM_WIDTH])
        mu = jnp.mean(v, axis=-1, keepdims=True)
        vc = v - mu
        v = vc * lax.rsqrt(jnp.mean(vc * vc, axis=-1, keepdims=True) + EPS)
        v = v * lng_ref[...] + lnb_ref[...]
        for j in range(GM_HEADS // 2):
            vp = v[:, j * LANES:(j + 1) * LANES]
            rhs = jnp.concatenate([jnp.where(lo_half, vp, 0.0), jnp.where(lo_half, 0.0, vp)], axis=0)
            mixed = _dot(ws_masked[j], rhs.astype(BF16)) + bs_ref[:, j * LANES:(j + 1) * LANES]
            mix_s[r0:r0 + L, j * LANES:(j + 1) * LANES] = (u[:, j * LANES:(j + 1) * LANES] * mixed).astype(BF16)

        qk = cb_ref[...] + cw_ref[0:1, :] * qk_s[r0 + TAIL - 3:r0 + TAIL - 3 + L, :]
        for j in range(1, QK_CONV):
            off = r0 + TAIL - (QK_CONV - 1) + j
            qk = qk + cw_ref[j:j + 1, :] * qk_s[off:off + L, :]
        qk = qk * jax.nn.sigmoid(qk)

        g = gates_s[r0:r0 + L, :]
        lf = jax.nn.log_sigmoid(g)
        hi, mid, lo = _split3_bf16(lf)
        bc = _dot(tri, hi) + _dot(tri, mid) + _dot(tri, lo)
        g_t = g.T
        bc_t = bc.T

        for hd in range(ML_HEADS):
            b_col = bc[:, ML_HEADS + hd:ML_HEADS + hd + 1]
            b_row = bc_t[ML_HEADS + hd:ML_HEADS + hd + 1, :]
            ig_col = g[:, hd:hd + 1]
            ig_row = g_t[hd:hd + 1, :]
            b_last = b_col[L - 1:L, :]
            m0 = m_s[hd][:, 0:1]

            q = qk[:, hd * dh:(hd + 1) * dh].astype(BF16)
            kf = qk[:, ML_WIDTH + hd * dh:ML_WIDTH + (hd + 1) * dh] * k_scale
            vb = vo_s[r0:r0 + L, hd * dh:(hd + 1) * dh].astype(BF16)
            og = vo_s[r0:r0 + L, ML_WIDTH + hd * dh:ML_WIDTH + (hd + 1) * dh]

            log_d = jnp.where(causal, b_col - b_row + ig_row, -jnp.inf)
            inter = b_col + m0
            m_t = jnp.maximum(jnp.max(log_d, axis=-1, keepdims=True), inter)
            w_d = jnp.exp(log_d - m_t)
            inter_w = jnp.exp(inter - m_t)

            s = lax.dot_general(q, kf.astype(BF16), (((1,), (1,)), ((), ())),
                                preferred_element_type=F32) * w_d
            c0 = c_s[hd]
            qc = _dot(q, c0.astype(BF16))
            num = inter_w * qc[:, 0:dh] + _dot(s.astype(BF16), vb)
            den = inter_w * qc[:, dh:dh + 1] + jnp.sum(s, axis=-1, keepdims=True)
            hr = num * (1.0 / jnp.maximum(jnp.abs(den), jnp.exp(-m_t)))
            hc = jax.nn.sigmoid(og) * hr
            hn = hc * lax.rsqrt(jnp.mean(hc * hc, axis=-1, keepdims=True) + EPS)
            hn = hn * ng_ref[:, hd * dh:(hd + 1) * dh]
            mix_s[r0:r0 + L, GM_WIDTH + hd * dh:GM_WIDTH + (hd + 1) * dh] = hn.astype(BF16)

            a_col = b_last - b_col + ig_col
            m_new = jnp.maximum(b_last + m0, jnp.max(a_col, axis=0, keepdims=True))
            decay = jnp.exp(b_last + m0 - m_new)
            kw = (kf * jnp.exp(a_col - m_new)).astype(BF16)
            v_aug = jnp.concatenate([vb, ones_col], axis=1)
            upd = lax.dot_general(kw, v_aug, (((0,), (0,)), ((), ())), preferred_element_type=F32)
            c_s[hd] = decay * c0 + upd
            m_s[hd] = jnp.broadcast_to(m_new, (1, LANES))

    qk_s[0:TAIL, :] = qk_s[tm:tm + TAIL, :]
    o_ref[0] = x + _dot(mix_s[...], wout_ref[...])


def _mix_layer(x, g, win, wgate, gbias, lng, lnb, ws, bs, cw, cb, ng, wout, *, tm):
    B, S, D = x.shape
    const = lambda shape: pl.BlockSpec(shape, lambda b, t: (0,) * len(shape), pipeline_mode=pl.Buffered(1))
    return pl.pallas_call(
        functools.partial(_mix_kernel, tm=tm),
        grid=(B, S // tm),
        in_specs=[
            pl.BlockSpec((1, tm, D), lambda b, t: (b, t, 0)),
            const(g.shape), const(win.shape), const(wgate.shape), const(gbias.shape),
            const(lng.shape), const(lnb.shape), const(ws.shape), const(bs.shape),
            const(cw.shape), const(cb.shape), const(ng.shape), const(wout.shape),
        ],
        out_specs=pl.BlockSpec((1, tm, D), lambda b, t: (b, t, 0)),
        out_shape=jax.ShapeDtypeStruct(x.shape, F32),
        scratch_shapes=[
            pltpu.VMEM((tm, 2 * GM_WIDTH), F32),
            pltpu.VMEM((tm + TAIL, 2 * ML_WIDTH), F32),
            pltpu.VMEM((tm, 2 * ML_WIDTH), F32),
            pltpu.VMEM((tm, LANES), F32),
            pltpu.VMEM((tm, GM_WIDTH + ML_WIDTH), BF16),
            pltpu.VMEM((ML_HEADS, ML_HEAD_DIM, 2 * ML_HEAD_DIM), F32),
            pltpu.VMEM((ML_HEADS, 1, LANES), F32),
        ],
        compiler_params=pltpu.CompilerParams(
            dimension_semantics=("arbitrary", "arbitrary"), vmem_limit_bytes=VMEM_LIMIT_BYTES),
        name="token_mix",
    )(x, g, win, wgate, gbias, lng, lnb, ws, bs, cw, cb, ng, wout)


def _ffn_kernel(x_ref, p_ref, gffn_ref, wup_ref, cw_ref, cb_ref, wdown_ref, gple_ref, wpg_ref,
                wple_ref, gfin_ref, o_ref, ubuf_s, tail_s, act_s, *, tm, final):
    nblk = wup_ref.shape[0] // 2

    @pl.when(pl.program_id(1) == 0)
    def _():
        tail_s[...] = jnp.zeros(tail_s.shape, F32)

    x = x_ref[0]
    h = _rms_norm(x, gffn_ref[...]).astype(BF16)
    for j in range(nblk):
        halves = []
        for part in range(2):
            idx = j + part * nblk
            up = _dot(h, wup_ref[idx])
            ubuf_s[part, 0:TAIL, :] = tail_s[idx]
            ubuf_s[part, TAIL:TAIL + tm, :] = up
            tail_s[idx] = up[tm - TAIL:tm, :]
            cv = cb_ref[idx] + cw_ref[idx, FFN_CONV - 1:FFN_CONV, :] * up
            for k in range(FFN_CONV - 1):
                off = TAIL - (FFN_CONV - 1) + k
                cv = cv + cw_ref[idx, k:k + 1, :] * ubuf_s[part, off:off + tm, :]
            halves.append(cv)
        a, bv = halves
        act_s[:, j * FFN_BLOCK:(j + 1) * FFN_BLOCK] = (a * jax.nn.sigmoid(a) * bv).astype(BF16)
    x1 = x + _dot(act_s[...], wdown_ref[...])

    gate = jax.nn.sigmoid(_dot(_rms_norm(x1, gple_ref[...]).astype(BF16), wpg_ref[...]))
    out = x1 + gate * _dot(p_ref[0, 0].astype(BF16), wple_ref[...])
    if final:
        out = _rms_norm(out, gfin_ref[...])
    o_ref[0] = out


def _ffn_layer(x, p, layer, gffn, wup, cw, cb, wdown, gple, wpg, wple, gfin, *, tm, final):
    B, S, D = x.shape
    const = lambda shape: pl.BlockSpec(shape, lambda b, t: (0,) * len(shape), pipeline_mode=pl.Buffered(1))
    return pl.pallas_call(
        functools.partial(_ffn_kernel, tm=tm, final=final),
        grid=(B, S // tm),
        in_specs=[
            pl.BlockSpec((1, tm, D), lambda b, t: (b, t, 0)),
            pl.BlockSpec((1, 1, tm, p.shape[-1]), lambda b, t: (layer, b, t, 0)),
            const(gffn.shape), const(wup.shape), const(cw.shape), const(cb.shape), const(wdown.shape),
            const(gple.shape), const(wpg.shape), const(wple.shape), const(gfin.shape),
        ],
        out_specs=pl.BlockSpec((1, tm, D), lambda b, t: (b, t, 0)),
        out_shape=jax.ShapeDtypeStruct(x.shape, F32),
        scratch_shapes=[
            pltpu.VMEM((2, tm + TAIL, FFN_BLOCK), F32),
            pltpu.VMEM((wup.shape[0], TAIL, FFN_BLOCK), F32),
            pltpu.VMEM((tm, wdown.shape[0]), BF16),
        ],
        compiler_params=pltpu.CompilerParams(
            dimension_semantics=("arbitrary", "arbitrary"), vmem_limit_bytes=VMEM_LIMIT_BYTES),
        name="channel_mix",
    )(x, p, gffn, wup, cw, cb, wdown, gple, wpg, wple, gfin)


def _column_blocks(w, width):
    *lead, k, n = w.shape
    w = w.reshape(*lead, k, n // width, width)
    return jnp.moveaxis(w, -2, -3)


def kernel(x, p, g_mix, w_in, gm_ln_g, gm_ln_b, gm_ws, gm_bs, ml_conv_w, ml_conv_b, ml_b_i, ml_b_f,
           ml_norm_g, w_out, g_ffn, w_up, ffn_conv_w, ffn_conv_b, w_down, g_ple, w_ple_gate, w_ple,
           g_final):
    depth, d_model = g_mix.shape
    d_ff = w_down.shape[1]
    tm = min(512, x.shape[1])
    n_main = 2 * GM_WIDTH + 4 * ML_WIDTH

    win = _column_blocks(w_in[:, :, :n_main].astype(BF16), 2 * GM_WIDTH)
    wgate = jnp.pad(w_in[:, :, n_main:], ((0, 0), (0, 0), (0, LANES - 2 * ML_HEADS))).astype(BF16)
    gbias = jnp.pad(jnp.concatenate([ml_b_i, ml_b_f], axis=-1), ((0, 0), (0, LANES - 2 * ML_HEADS)))
    ws = gm_ws.reshape(depth, GM_HEADS // 2, 2, CHUNK, CHUNK).transpose(0, 1, 3, 2, 4)
    ws = ws.reshape(depth, GM_HEADS // 2, CHUNK, 2 * CHUNK)
    bs = jnp.repeat(jnp.swapaxes(gm_bs, 1, 2), GM_HEAD_DIM, axis=2)
    wout = w_out.astype(BF16)
    wup = _column_blocks(w_up.astype(BF16), FFN_BLOCK)
    fcw = jnp.swapaxes(ffn_conv_w.reshape(depth, FFN_CONV, 2 * d_ff // FFN_BLOCK, FFN_BLOCK), 1, 2)
    fcb = ffn_conv_b.reshape(depth, 2 * d_ff // FFN_BLOCK, 1, FFN_BLOCK)
    wdown = w_down.astype(BF16)
    wpg = w_ple_gate.astype(BF16)
    wple = w_ple.astype(BF16)
    row = lambda a: a.reshape(1, -1)

    for i in range(depth):
        x = _mix_layer(x, row(g_mix[i]), win[i], wgate[i], row(gbias[i]), row(gm_ln_g[i]), row(gm_ln_b[i]),
                       ws[i], bs[i], ml_conv_w[i], row(ml_conv_b[i]), row(ml_norm_g[i]), wout[i], tm=tm)
        x = _ffn_layer(x, p, i, row(g_ffn[i]), wup[i], fcw[i], fcb[i], wdown[i], row(g_ple[i]), wpg[i],
                       wple[i], row(g_final), tm=tm, final=(i == depth - 1))
    return x
```

```python
import functools

import jax
import jax.numpy as jnp
from jax import lax
from jax.experimental import pallas as pl
from jax.experimental.pallas import tpu as pltpu

EPS = 1e-6
CHUNK = 128
GM_HEADS, GM_HEAD_DIM = 8, 64
GM_WIDTH = GM_HEADS * GM_HEAD_DIM
ML_HEADS, ML_HEAD_DIM = 4, 128
ML_WIDTH = ML_HEADS * ML_HEAD_DIM
QK_CONV, FFN_CONV = 4, 3
TAIL = 8
LANES = 128
MXU_N = 256
MIX_TILE, MIX_SUBTILE = 512, 256
FFN_TILE = 512
VMEM_LIMIT_BYTES = 56 * 1024 * 1024

F32 = jnp.float32
BF16 = jnp.bfloat16


def _dot(a, b):
    return jnp.dot(a, b, preferred_element_type=F32)


def _rms_norm(x, g):
    return x * lax.rsqrt(jnp.mean(x * x, axis=-1, keepdims=True) + EPS) * g


def _gelu(x):
    return 0.5 * x * (1.0 + lax.erf(x * (2.0 ** -0.5)))


def _split3_bf16(x):
    hi = x.astype(BF16)
    r = x - hi.astype(F32)
    mid = r.astype(BF16)
    lo = (r - mid.astype(F32)).astype(BF16)
    return hi, mid, lo


def _layer_spec(layer, shape):
    zeros = (0,) * len(shape)
    return pl.BlockSpec((None, *shape), lambda b, t: (layer, *zeros), pipeline_mode=pl.Buffered(1))


def _column_block_specs(layer, rows, n_blocks, first=0):
    return [pl.BlockSpec((None, rows, MXU_N), lambda b, t, n=first + n: (layer, 0, n),
                         pipeline_mode=pl.Buffered(1)) for n in range(n_blocks)]


def _const_spec(shape):
    return pl.BlockSpec(shape, lambda b, t: (0,) * len(shape), pipeline_mode=pl.Buffered(1))


def _mix_kernel(*refs, tm, sub, n_in, n_out):
    x_ref, g_ref = refs[0:2]
    win_refs = refs[2:2 + n_in]
    (wgate_ref, gbias_ref, lng_ref, lnb_ref, ws_ref, bs_ref, cw_ref, cb_ref,
     ng_ref) = refs[2 + n_in:11 + n_in]
    wout_refs = refs[11 + n_in:11 + n_in + n_out]
    tri_ref, aug_ref, o_ref = refs[11 + n_in + n_out:14 + n_in + n_out]
    uv_s, qk_s, vo_s, gates_s, mix_s, c_s, m_s, h_s = refs[14 + n_in + n_out:]
    L, dh = CHUNK, ML_HEAD_DIM
    per = n_in // 3

    @pl.when(pl.program_id(1) == 0)
    def _():
        qk_s[:, 0:TAIL, :] = jnp.zeros((qk_s.shape[0], TAIL, LANES), F32)
        c_s[...] = jnp.zeros(c_s.shape, F32)
        m_s[...] = jnp.zeros(m_s.shape, F32)

    def project_pieces(k):
        rows = slice(k * sub, (k + 1) * sub)

        def norm():
            h_s[...] = _rms_norm(x_ref[0, rows, :], g_ref[...]).astype(BF16)

        def block(i, n):
            def run():
                res = _dot(h_s[...], win_refs[i * per + n][...])
                cols = slice(n * MXU_N, (n + 1) * MXU_N)
                if i == 0:
                    uv_s[rows, cols] = res
                elif i == 2:
                    vo_s[rows, cols] = res
                else:
                    for half in range(MXU_N // LANES):
                        slab = n * (MXU_N // LANES) + half
                        qk_s[slab, TAIL + k * sub:TAIL + (k + 1) * sub, :] = res[:, half * LANES:(half + 1) * LANES]
            return run

        def gates():
            gates_s[rows, :] = _dot(h_s[...], wgate_ref[...]) + gbias_ref[...]

        return [norm] + [block(i, n) for i in range(3) for n in range(per)] + [gates]

    def out_project_pieces(k):
        rows = slice(k * sub, (k + 1) * sub)

        def block(n):
            def run():
                cols = slice(n * MXU_N, (n + 1) * MXU_N)
                o_ref[0, rows, cols] = x_ref[0, rows, cols] + _dot(mix_s[rows, :], wout_refs[n][...])
            return run

        return [block(n) for n in range(n_out)]

    row = lax.broadcasted_iota(jnp.int32, (L, L), 0)
    col = lax.broadcasted_iota(jnp.int32, (L, L), 1)
    causal = col <= row
    lo_half = lax.broadcasted_iota(jnp.int32, (L, LANES), 1) < GM_HEAD_DIM
    causal2 = jnp.concatenate([causal, causal], axis=1)
    ws_masked = [jnp.where(causal2, ws_ref[j], 0.0).astype(BF16) for j in range(GM_HEADS // 2)]
    k_scale = dh ** -0.5

    def mix_chunk(c):
        r0 = c * L
        u = _gelu(uv_s[r0:r0 + L, 0:GM_WIDTH])
        v = _gelu(uv_s[r0:r0 + L, GM_WIDTH:2 * GM_WIDTH])
        mu = jnp.mean(v, axis=-1, keepdims=True)
        vc = v - mu
        v = vc * lax.rsqrt(jnp.mean(vc * vc, axis=-1, keepdims=True) + EPS)
        v = v * lng_ref[...] + lnb_ref[...]
        for j in range(GM_HEADS // 2):
            vp = v[:, j * LANES:(j + 1) * LANES]
            rhs = jnp.concatenate([jnp.where(lo_half, vp, 0.0), jnp.where(lo_half, 0.0, vp)], axis=0)
            mixed = _dot(ws_masked[j], rhs.astype(BF16)) + bs_ref[:, j * LANES:(j + 1) * LANES]
            mix_s[r0:r0 + L, j * LANES:(j + 1) * LANES] = (u[:, j * LANES:(j + 1) * LANES] * mixed).astype(BF16)
        yield

        def conv_silu(slab):
            cols = slice(slab * LANES, (slab + 1) * LANES)
            acc = cb_ref[:, cols]
            for j in range(QK_CONV):
                off = r0 + TAIL - (QK_CONV - 1) + j
                acc = acc + cw_ref[j:j + 1, cols] * qk_s[slab, off:off + L, :]
            return acc * jax.nn.sigmoid(acc)

        g = gates_s[r0:r0 + L, :]
        lf = jax.nn.log_sigmoid(g)
        hi, mid, lo = _split3_bf16(lf)
        tri = tri_ref[...]
        bc = _dot(tri, hi) + _dot(tri, mid) + _dot(tri, lo)
        g_t = g.T
        bc_t = bc.T

        for hd in range(ML_HEADS):
            yield
            b_col = bc[:, ML_HEADS + hd:ML_HEADS + hd + 1]
            b_row = bc_t[ML_HEADS + hd:ML_HEADS + hd + 1, :]
            ig_col = g[:, hd:hd + 1]
            ig_row = g_t[hd:hd + 1, :]
            b_last = b_col[L - 1:L, :]
            m0 = m_s[hd][:, 0:1]

            q = conv_silu(hd).astype(BF16)
            kf = conv_silu(ML_HEADS + hd) * k_scale
            vb = vo_s[r0:r0 + L, hd * dh:(hd + 1) * dh].astype(BF16)
            og = vo_s[r0:r0 + L, ML_WIDTH + hd * dh:ML_WIDTH + (hd + 1) * dh]

            log_d = jnp.where(causal, b_col - b_row + ig_row, -jnp.inf)
            inter = b_col + m0
            m_t = jnp.maximum(jnp.max(log_d, axis=-1, keepdims=True), inter)
            w_d = jnp.exp(log_d - m_t)
            inter_w = jnp.exp(inter - m_t)

            s = lax.dot_general(q, kf.astype(BF16), (((1,), (1,)), ((), ())),
                                preferred_element_type=F32) * w_d
            c0 = c_s[hd]
            qc = _dot(q, c0.astype(BF16))
            num = inter_w * qc[:, 0:dh] + _dot(s.astype(BF16), vb)
            den = inter_w * qc[:, dh:dh + 1] + jnp.sum(s, axis=-1, keepdims=True)
            hr = num * (1.0 / jnp.maximum(jnp.abs(den), jnp.exp(-m_t)))
            hc = jax.nn.sigmoid(og) * hr
            hn = hc * lax.rsqrt(jnp.mean(hc * hc, axis=-1, keepdims=True) + EPS)
            hn = hn * ng_ref[:, hd * dh:(hd + 1) * dh]
            mix_s[r0:r0 + L, GM_WIDTH + hd * dh:GM_WIDTH + (hd + 1) * dh] = hn.astype(BF16)

            a_col = b_last - b_col + ig_col
            m_new = jnp.maximum(b_last + m0, jnp.max(a_col, axis=0, keepdims=True))
            decay = jnp.exp(b_last + m0 - m_new)
            kw = (kf * jnp.exp(a_col - m_new)).astype(BF16)
            v_aug = jnp.concatenate([vb, aug_ref[...]], axis=1)
            upd = lax.dot_general(kw, v_aug, (((0,), (0,)), ((), ())), preferred_element_type=F32)
            c_s[hd] = decay * c0 + upd
            m_s[hd] = jnp.broadcast_to(m_new, (1, LANES))

    n_sub = tm // sub
    sections_per_chunk = 2 + ML_HEADS
    n_sections = sections_per_chunk * sub // L
    for piece in project_pieces(0):
        piece()
    for k in range(n_sub):
        pieces = project_pieces(k + 1) if k + 1 < n_sub else []
        pieces += out_project_pieces(k - 1) if k > 0 else []
        done = 0
        for ci, c in enumerate(range(k * sub // L, (k + 1) * sub // L)):
            chunk = mix_chunk(c)
            for si in range(sections_per_chunk):
                upto = -(-(ci * sections_per_chunk + si + 1) * len(pieces) // n_sections)
                for piece in pieces[done:upto]:
                    piece()
                done = upto
                next(chunk, None)
    for piece in out_project_pieces(n_sub - 1):
        piece()
    qk_s[:, 0:TAIL, :] = qk_s[:, tm:tm + TAIL, :]


def _mix_layer(x, layer, g, win, wgate, gbias, lng, lnb, ws, bs, cw, cb, ng, wout, tri, aug, *, tm, sub):
    B, S, D = x.shape
    n_in = (2 * GM_WIDTH + 4 * ML_WIDTH) // MXU_N
    n_out = wout.shape[2] // MXU_N
    lspec = lambda a: _layer_spec(layer, a.shape[1:])
    return pl.pallas_call(
        functools.partial(_mix_kernel, tm=tm, sub=sub, n_in=n_in, n_out=n_out),
        grid=(B, S // tm),
        in_specs=[pl.BlockSpec((1, tm, D), lambda b, t: (b, t, 0)), lspec(g)]
        + _column_block_specs(layer, D, n_in)
        + [lspec(a) for a in (wgate, gbias, lng, lnb, ws, bs, cw, cb, ng)]
        + _column_block_specs(layer, wout.shape[1], n_out)
        + [_const_spec(tri.shape), _const_spec(aug.shape)],
        out_specs=pl.BlockSpec((1, tm, D), lambda b, t: (b, t, 0)),
        out_shape=jax.ShapeDtypeStruct(x.shape, F32),
        scratch_shapes=[
            pltpu.VMEM((tm, 2 * GM_WIDTH), F32),
            pltpu.VMEM((2 * ML_WIDTH // LANES, tm + TAIL, LANES), F32),
            pltpu.VMEM((tm, 2 * ML_WIDTH), F32),
            pltpu.VMEM((tm, LANES), F32),
            pltpu.VMEM((tm, GM_WIDTH + ML_WIDTH), BF16),
            pltpu.VMEM((ML_HEADS, ML_HEAD_DIM, 2 * ML_HEAD_DIM), F32),
            pltpu.VMEM((ML_HEADS, 1, LANES), F32),
            pltpu.VMEM((sub, D), BF16),
        ],
        compiler_params=pltpu.CompilerParams(
            dimension_semantics=("arbitrary", "arbitrary"), vmem_limit_bytes=VMEM_LIMIT_BYTES),
        name="token_mix",
    )(x, g, *([win] * n_in), wgate, gbias, lng, lnb, ws, bs, cw, cb, ng, *([wout] * n_out), tri, aug)


def _ffn_kernel(*refs, tm, final, n_up, n_down):
    x_ref, p_ref, gffn_ref = refs[0:3]
    wup_refs = refs[3:3 + n_up]
    cw_ref, cb_ref = refs[3 + n_up:5 + n_up]
    wdown_refs = refs[5 + n_up:5 + n_up + n_down]
    gple_ref, wpg_ref, wple_ref, gfin_ref, o_ref = refs[5 + n_up + n_down:10 + n_up + n_down]
    ubuf_s, tail_s, act_s, x1_s = refs[10 + n_up + n_down:]
    nblk = n_up // 2

    @pl.when(pl.program_id(1) == 0)
    def _():
        tail_s[...] = jnp.zeros(tail_s.shape, F32)

    h = _rms_norm(x_ref[0], gffn_ref[...]).astype(BF16)
    for j in range(nblk):
        halves = []
        for part in range(2):
            idx = j + part * nblk
            cols = slice(idx * MXU_N, (idx + 1) * MXU_N)
            up = _dot(h, wup_refs[idx][...])
            ubuf_s[part, 0:TAIL, :] = tail_s[idx]
            ubuf_s[part, TAIL:TAIL + tm, :] = up
            tail_s[idx] = up[tm - TAIL:tm, :]
            cv = cb_ref[:, cols] + cw_ref[FFN_CONV - 1:FFN_CONV, cols] * up
            for k in range(FFN_CONV - 1):
                off = TAIL - (FFN_CONV - 1) + k
                cv = cv + cw_ref[k:k + 1, cols] * ubuf_s[part, off:off + tm, :]
            halves.append(cv)
        a, bv = halves
        act_s[:, j * MXU_N:(j + 1) * MXU_N] = (a * jax.nn.sigmoid(a) * bv).astype(BF16)
    for n in range(n_down):
        cols = slice(n * MXU_N, (n + 1) * MXU_N)
        x1_s[:, cols] = x_ref[0, :, cols] + _dot(act_s[...], wdown_refs[n][...])

    x1 = x1_s[...]
    gate = jax.nn.sigmoid(_dot(_rms_norm(x1, gple_ref[...]).astype(BF16), wpg_ref[...]))
    out = x1 + gate * _dot(p_ref[0].astype(BF16), wple_ref[...])
    if final:
        out = _rms_norm(out, gfin_ref[...])
    o_ref[0] = out


def _ffn_layer(x, p, layer, gffn, wup, cw, cb, wdown, gple, wpg, wple, gfin, *, tm, final):
    B, S, D = x.shape
    d_ff = wdown.shape[1]
    n_up = 2 * d_ff // MXU_N
    n_down = D // MXU_N
    lspec = lambda a: _layer_spec(layer, a.shape[1:])
    return pl.pallas_call(
        functools.partial(_ffn_kernel, tm=tm, final=final, n_up=n_up, n_down=n_down),
        grid=(B, S // tm),
        in_specs=[pl.BlockSpec((1, tm, D), lambda b, t: (b, t, 0)),
                  pl.BlockSpec((None, 1, tm, p.shape[-1]), lambda b, t: (layer, b, t, 0)),
                  lspec(gffn)]
        + _column_block_specs(layer, D, n_up)
        + [lspec(cw), lspec(cb)]
        + _column_block_specs(layer, d_ff, n_down)
        + [lspec(gple), lspec(wpg), lspec(wple), _const_spec(gfin.shape)],
        out_specs=pl.BlockSpec((1, tm, D), lambda b, t: (b, t, 0)),
        out_shape=jax.ShapeDtypeStruct(x.shape, F32),
        scratch_shapes=[
            pltpu.VMEM((2, tm + TAIL, MXU_N), F32),
            pltpu.VMEM((n_up, TAIL, MXU_N), F32),
            pltpu.VMEM((tm, d_ff), BF16),
            pltpu.VMEM((tm, D), F32),
        ],
        compiler_params=pltpu.CompilerParams(
            dimension_semantics=("arbitrary", "arbitrary"), vmem_limit_bytes=VMEM_LIMIT_BYTES),
        name="channel_mix",
    )(x, p, gffn, *([wup] * n_up), cw, cb, *([wdown] * n_down), gple, wpg, wple, gfin)


def kernel(x, p, g_mix, w_in, gm_ln_g, gm_ln_b, gm_ws, gm_bs, ml_conv_w, ml_conv_b, ml_b_i, ml_b_f,
           ml_norm_g, w_out, g_ffn, w_up, ffn_conv_w, ffn_conv_b, w_down, g_ple, w_ple_gate, w_ple,
           g_final):
    depth = g_mix.shape[0]
    seq = x.shape[1]
    tm_mix, tm_ffn = min(MIX_TILE, seq), min(FFN_TILE, seq)
    sub = min(MIX_SUBTILE, tm_mix)
    n_main = 2 * GM_WIDTH + 4 * ML_WIDTH
    n_gates = 2 * ML_HEADS

    rows = lambda a: a[:, None, :]
    win = w_in.astype(BF16)
    wgate = jnp.pad(w_in[:, :, n_main:], ((0, 0), (0, 0), (0, LANES - n_gates))).astype(BF16)
    gbias = rows(jnp.pad(jnp.concatenate([ml_b_i, ml_b_f], axis=-1), ((0, 0), (0, LANES - n_gates))))
    ws = gm_ws.reshape(depth, GM_HEADS // 2, 2, CHUNK, CHUNK).transpose(0, 1, 3, 2, 4)
    ws = ws.reshape(depth, GM_HEADS // 2, CHUNK, 2 * CHUNK)
    bs = jnp.repeat(jnp.swapaxes(gm_bs, 1, 2), GM_HEAD_DIM, axis=2)
    tri = jnp.tri(CHUNK, dtype=BF16)
    aug = jnp.zeros((CHUNK, ML_HEAD_DIM), BF16).at[:, 0].set(1)
    wout, wup, wdown = w_out.astype(BF16), w_up.astype(BF16), w_down.astype(BF16)
    wpg, wple = w_ple_gate.astype(BF16), w_ple.astype(BF16)

    for i in range(depth):
        x = _mix_layer(x, i, rows(g_mix), win, wgate, gbias, rows(gm_ln_g), rows(gm_ln_b), ws, bs,
                       ml_conv_w, rows(ml_conv_b), rows(ml_norm_g), wout, tri, aug, tm=tm_mix, sub=sub)
        x = _ffn_layer(x, p, i, rows(g_ffn), wup, ffn_conv_w, rows(ffn_conv_b), wdown, rows(g_ple), wpg,
                       wple, g_final.reshape(1, -1), tm=tm_ffn, final=(i == depth - 1))
    return x
```

```python
import functools

import jax
import jax.numpy as jnp
from jax import lax
from jax.experimental import pallas as pl
from jax.experimental.pallas import tpu as pltpu

EPS = 1e-6
CHUNK = 128
GM_HEADS, GM_HEAD_DIM = 8, 64
GM_WIDTH = GM_HEADS * GM_HEAD_DIM
ML_HEADS, ML_HEAD_DIM = 4, 128
ML_WIDTH = ML_HEADS * ML_HEAD_DIM
QK_CONV, FFN_CONV = 4, 3
TAIL = 8
LANES = 128
MXU_N = 256
MIX_TILE = 512
FFN_TILE = 512
VMEM_LIMIT_BYTES = 56 * 1024 * 1024

F32 = jnp.float32
BF16 = jnp.bfloat16


def _dot(a, b):
    return jnp.dot(a, b, preferred_element_type=F32)


def _rms_norm(x, g):
    return x * lax.rsqrt(jnp.mean(x * x, axis=-1, keepdims=True) + EPS) * g


def _gelu(x):
    return 0.5 * x * (1.0 + lax.erf(x * (2.0 ** -0.5)))


def _split3_bf16(x):
    hi = x.astype(BF16)
    r = x - hi.astype(F32)
    mid = r.astype(BF16)
    lo = (r - mid.astype(F32)).astype(BF16)
    return hi, mid, lo


def _layer_spec(layer, shape):
    zeros = (0,) * len(shape)
    return pl.BlockSpec((None, *shape), lambda *_: (layer, *zeros), pipeline_mode=pl.Buffered(1))


def _column_block_specs(layer, rows, n_blocks, first=0):
    return [pl.BlockSpec((None, rows, MXU_N), lambda *_, n=first + n: (layer, 0, n),
                         pipeline_mode=pl.Buffered(1)) for n in range(n_blocks)]


def _const_spec(shape):
    return pl.BlockSpec(shape, lambda *_: (0,) * len(shape), pipeline_mode=pl.Buffered(1))


def _mix_kernel(*refs, tm, n_in, n_out, tiles_per_seq):
    x_ref, xlag_ref, g_ref = refs[0:3]
    win_refs = refs[3:3 + n_in]
    (wgate_ref, gbias_ref, lng_ref, lnb_ref, ws_ref, bs_ref, cw_ref, cb_ref,
     ng_ref) = refs[3 + n_in:12 + n_in]
    wout_refs = refs[12 + n_in:12 + n_in + n_out]
    tri_ref, aug_ref, o_ref = refs[12 + n_in + n_out:15 + n_in + n_out]
    scratch = refs[15 + n_in + n_out:]
    slots = (scratch[0:4], scratch[4:8])
    mixes = scratch[8:10]
    c_s, m_s, h_s = scratch[10:13]
    L, dh = CHUNK, ML_HEAD_DIM
    per = n_in // 3
    step = pl.program_id(0)

    @pl.when(step == 0)
    def _():
        for buf in (*slots[1], mixes[0], c_s, m_s):
            buf[...] = jnp.zeros(buf.shape, buf.dtype)

    @pl.when((step >= 1) & (lax.rem(step - 1, tiles_per_seq) == 0))
    def _():
        for uv_s, qk_s, vo_s, gates_s in slots:
            qk_s[:, 0:TAIL, :] = jnp.zeros((qk_s.shape[0], TAIL, LANES), F32)
        c_s[...] = jnp.zeros(c_s.shape, F32)
        m_s[...] = jnp.zeros(m_s.shape, F32)

    def project_pieces(dst):
        uv_s, qk_s, vo_s, gates_s = dst

        def norm():
            h_s[...] = _rms_norm(x_ref[0], g_ref[...]).astype(BF16)

        def block(i, n):
            def run():
                res = _dot(h_s[...], win_refs[i * per + n][...])
                cols = slice(n * MXU_N, (n + 1) * MXU_N)
                if i == 0:
                    uv_s[:, cols] = res
                elif i == 2:
                    vo_s[:, cols] = res
                else:
                    for half in range(MXU_N // LANES):
                        qk_s[n * (MXU_N // LANES) + half, TAIL:TAIL + tm, :] = res[:, half * LANES:(half + 1) * LANES]
            return run

        def gates():
            gates_s[...] = _dot(h_s[...], wgate_ref[...]) + gbias_ref[...]

        return [norm] + [block(i, n) for i in range(3) for n in range(per)] + [gates]

    def out_project_pieces(mix_s):
        def block(n):
            def run():
                cols = slice(n * MXU_N, (n + 1) * MXU_N)
                o_ref[0, :, cols] = xlag_ref[0, :, cols] + _dot(mix_s[...], wout_refs[n][...])
            return run

        return [block(n) for n in range(n_out)]

    row = lax.broadcasted_iota(jnp.int32, (L, L), 0)
    col = lax.broadcasted_iota(jnp.int32, (L, L), 1)
    causal = col <= row
    lo_half = lax.broadcasted_iota(jnp.int32, (L, LANES), 1) < GM_HEAD_DIM
    causal2 = jnp.concatenate([causal, causal], axis=1)
    k_scale = dh ** -0.5

    def mix_chunk(c, src, mix_s, ws_masked):
        uv_s, qk_s, vo_s, gates_s = src
        r0 = c * L
        u = _gelu(uv_s[r0:r0 + L, 0:GM_WIDTH])
        v = _gelu(uv_s[r0:r0 + L, GM_WIDTH:2 * GM_WIDTH])
        mu = jnp.mean(v, axis=-1, keepdims=True)
        vc = v - mu
        v = vc * lax.rsqrt(jnp.mean(vc * vc, axis=-1, keepdims=True) + EPS)
        v = v * lng_ref[...] + lnb_ref[...]
        for j in range(GM_HEADS // 2):
            vp = v[:, j * LANES:(j + 1) * LANES]
            rhs = jnp.concatenate([jnp.where(lo_half, vp, 0.0), jnp.where(lo_half, 0.0, vp)], axis=0)
            mixed = _dot(ws_masked[j], rhs.astype(BF16)) + bs_ref[:, j * LANES:(j + 1) * LANES]
            mix_s[r0:r0 + L, j * LANES:(j + 1) * LANES] = (u[:, j * LANES:(j + 1) * LANES] * mixed).astype(BF16)
        yield

        def conv_silu(slab):
            cols = slice(slab * LANES, (slab + 1) * LANES)
            acc = cb_ref[:, cols]
            for j in range(QK_CONV):
                off = r0 + TAIL - (QK_CONV - 1) + j
                acc = acc + cw_ref[j:j + 1, cols] * qk_s[slab, off:off + L, :]
            return acc * jax.nn.sigmoid(acc)

        g = gates_s[r0:r0 + L, :]
        lf = jax.nn.log_sigmoid(g)
        hi, mid, lo = _split3_bf16(lf)
        tri = tri_ref[...]
        bc = _dot(tri, hi) + _dot(tri, mid) + _dot(tri, lo)
        g_t = g.T
        bc_t = bc.T

        for hd in range(ML_HEADS):
            yield
            b_col = bc[:, ML_HEADS + hd:ML_HEADS + hd + 1]
            b_row = bc_t[ML_HEADS + hd:ML_HEADS + hd + 1, :]
            ig_col = g[:, hd:hd + 1]
            ig_row = g_t[hd:hd + 1, :]
            b_last = b_col[L - 1:L, :]
            m0 = m_s[hd][:, 0:1]

            q = conv_silu(hd).astype(BF16)
            kf = conv_silu(ML_HEADS + hd) * k_scale
            vb = vo_s[r0:r0 + L, hd * dh:(hd + 1) * dh].astype(BF16)
            og = vo_s[r0:r0 + L, ML_WIDTH + hd * dh:ML_WIDTH + (hd + 1) * dh]

            log_d = jnp.where(causal, b_col - b_row + ig_row, -jnp.inf)
            inter = b_col + m0
            m_t = jnp.maximum(jnp.max(log_d, axis=-1, keepdims=True), inter)
            w_d = jnp.exp(log_d - m_t)
            inter_w = jnp.exp(inter - m_t)

            s = lax.dot_general(q, kf.astype(BF16), (((1,), (1,)), ((), ())),
                                preferred_element_type=F32) * w_d
            c0 = c_s[hd]
            qc = _dot(q, c0.astype(BF16))
            num = inter_w * qc[:, 0:dh] + _dot(s.astype(BF16), vb)
            den = inter_w * qc[:, dh:dh + 1] + jnp.sum(s, axis=-1, keepdims=True)
            hr = num * (1.0 / jnp.maximum(jnp.abs(den), jnp.exp(-m_t)))
            hc = jax.nn.sigmoid(og) * hr
            hn = hc * lax.rsqrt(jnp.mean(hc * hc, axis=-1, keepdims=True) + EPS)
            hn = hn * ng_ref[:, hd * dh:(hd + 1) * dh]
            mix_s[r0:r0 + L, GM_WIDTH + hd * dh:GM_WIDTH + (hd + 1) * dh] = hn.astype(BF16)

            a_col = b_last - b_col + ig_col
            m_new = jnp.maximum(b_last + m0, jnp.max(a_col, axis=0, keepdims=True))
            decay = jnp.exp(b_last + m0 - m_new)
            kw = (kf * jnp.exp(a_col - m_new)).astype(BF16)
            v_aug = jnp.concatenate([vb, aug_ref[...]], axis=1)
            upd = lax.dot_general(kw, v_aug, (((0,), (0,)), ((), ())), preferred_element_type=F32)
            c_s[hd] = decay * c0 + upd
            m_s[hd] = jnp.broadcast_to(m_new, (1, LANES))

    def stages(a, b):
        ws_masked = [jnp.where(causal2, ws_ref[j], 0.0).astype(BF16) for j in range(GM_HEADS // 2)]
        pieces = project_pieces(slots[a]) + out_project_pieces(mixes[a])
        sections_per_chunk = 2 + ML_HEADS
        n_sections = sections_per_chunk * tm // L
        done = 0
        for c in range(tm // L):
            chunk = mix_chunk(c, slots[b], mixes[b], ws_masked)
            for si in range(sections_per_chunk):
                upto = -(-(c * sections_per_chunk + si + 1) * len(pieces) // n_sections)
                upto = min(len(pieces), -(-upto // 2) * 2)
                for piece in pieces[done:upto]:
                    piece()
                done = upto
                next(chunk, None)
        slots[a][1][:, 0:TAIL, :] = slots[b][1][:, tm:tm + TAIL, :]

    parity = lax.rem(step, 2)
    pl.when(parity == 0)(lambda: stages(0, 1))
    pl.when(parity == 1)(lambda: stages(1, 0))


def _mix_layer(x, layer, g, win, wgate, gbias, lng, lnb, ws, bs, cw, cb, ng, wout, tri, aug, *, tm):
    B, S, D = x.shape
    n_in = (2 * GM_WIDTH + 4 * ML_WIDTH) // MXU_N
    n_out = wout.shape[2] // MXU_N
    tiles_per_seq = S // tm
    n_tiles = B * tiles_per_seq
    lspec = lambda a: _layer_spec(layer, a.shape[1:])

    def tile_spec(lag):
        def index(s):
            tile = jnp.clip(s - lag, 0, n_tiles - 1)
            return (tile // tiles_per_seq, tile % tiles_per_seq, 0)
        return pl.BlockSpec((1, tm, D), index)

    slot = [
        pltpu.VMEM((tm, 2 * GM_WIDTH), F32),
        pltpu.VMEM((2 * ML_WIDTH // LANES, tm + TAIL, LANES), F32),
        pltpu.VMEM((tm, 2 * ML_WIDTH), F32),
        pltpu.VMEM((tm, LANES), F32),
    ]
    return pl.pallas_call(
        functools.partial(_mix_kernel, tm=tm, n_in=n_in, n_out=n_out, tiles_per_seq=tiles_per_seq),
        grid=(n_tiles + 2,),
        in_specs=[tile_spec(0), tile_spec(2), lspec(g)]
        + _column_block_specs(layer, D, n_in)
        + [lspec(a) for a in (wgate, gbias, lng, lnb, ws, bs, cw, cb, ng)]
        + _column_block_specs(layer, wout.shape[1], n_out)
        + [_const_spec(tri.shape), _const_spec(aug.shape)],
        out_specs=tile_spec(2),
        out_shape=jax.ShapeDtypeStruct(x.shape, F32),
        scratch_shapes=slot + slot + [
            pltpu.VMEM((tm, GM_WIDTH + ML_WIDTH), BF16),
            pltpu.VMEM((tm, GM_WIDTH + ML_WIDTH), BF16),
            pltpu.VMEM((ML_HEADS, ML_HEAD_DIM, 2 * ML_HEAD_DIM), F32),
            pltpu.VMEM((ML_HEADS, 1, LANES), F32),
            pltpu.VMEM((tm, D), BF16),
        ],
        compiler_params=pltpu.CompilerParams(
            dimension_semantics=("arbitrary",), vmem_limit_bytes=VMEM_LIMIT_BYTES),
        name="token_mix",
    )(x, x, g, *([win] * n_in), wgate, gbias, lng, lnb, ws, bs, cw, cb, ng, *([wout] * n_out), tri, aug)


def _ffn_kernel(*refs, tm, final, n_up, n_down):
    x_ref, p_ref, gffn_ref = refs[0:3]
    wup_refs = refs[3:3 + n_up]
    cw_ref, cb_ref = refs[3 + n_up:5 + n_up]
    wdown_refs = refs[5 + n_up:5 + n_up + n_down]
    gple_ref = refs[5 + n_up + n_down]
    wpg_refs = refs[6 + n_up + n_down:6 + n_up + 2 * n_down]
    wple_refs = refs[6 + n_up + 2 * n_down:6 + n_up + 3 * n_down]
    gfin_ref, o_ref = refs[6 + n_up + 3 * n_down:8 + n_up + 3 * n_down]
    ubuf_s, act_s, x1_s, h2_s, pe_s = refs[8 + n_up + 3 * n_down:]
    nblk = n_up // 2
    slabs = MXU_N // LANES

    @pl.when(pl.program_id(1) == 0)
    def _():
        ubuf_s[:, :, 0:TAIL, :] = jnp.zeros((ubuf_s.shape[0], ubuf_s.shape[1], TAIL, LANES), F32)

    pb = p_ref[0].astype(BF16)
    for n in range(n_down):
        cols = slice(n * MXU_N, (n + 1) * MXU_N)
        pe_s[:, cols] = _dot(pb, wple_refs[n][...])
    h = _rms_norm(x_ref[0], gffn_ref[...]).astype(BF16)
    for j in range(nblk):
        halves = []
        for part in range(2):
            idx = j + part * nblk
            up = _dot(h, wup_refs[idx][...])
            cv = []
            for sl in range(slabs):
                cols = slice(idx * MXU_N + sl * LANES, idx * MXU_N + (sl + 1) * LANES)
                ubuf_s[idx, sl, TAIL:TAIL + tm, :] = up[:, sl * LANES:(sl + 1) * LANES]
                acc = cb_ref[:, cols]
                for k in range(FFN_CONV):
                    off = TAIL - (FFN_CONV - 1) + k
                    acc = acc + cw_ref[k:k + 1, cols] * ubuf_s[idx, sl, off:off + tm, :]
                ubuf_s[idx, sl, 0:TAIL, :] = ubuf_s[idx, sl, tm:tm + TAIL, :]
                cv.append(acc)
            halves.append(jnp.concatenate(cv, axis=1))
        a, bv = halves
        act_s[:, j * MXU_N:(j + 1) * MXU_N] = (a * jax.nn.sigmoid(a) * bv).astype(BF16)

    for n in range(n_down):
        cols = slice(n * MXU_N, (n + 1) * MXU_N)
        x1_s[:, cols] = x_ref[0, :, cols] + _dot(act_s[...], wdown_refs[n][...])

    h2_s[...] = _rms_norm(x1_s[...], gple_ref[...]).astype(BF16)
    for n in range(n_down):
        cols = slice(n * MXU_N, (n + 1) * MXU_N)
        gate = jax.nn.sigmoid(_dot(h2_s[...], wpg_refs[n][...]))
        o_ref[0, :, cols] = x1_s[:, cols] + gate * pe_s[:, cols]
    if final:
        o_ref[0] = _rms_norm(o_ref[0], gfin_ref[...])


def _ffn_layer(x, p, layer, gffn, wup, cw, cb, wdown, gple, wpg, wple, gfin, *, tm, final):
    B, S, D = x.shape
    d_ff = wdown.shape[1]
    n_up = 2 * d_ff // MXU_N
    n_down = D // MXU_N
    lspec = lambda a: _layer_spec(layer, a.shape[1:])
    return pl.pallas_call(
        functools.partial(_ffn_kernel, tm=tm, final=final, n_up=n_up, n_down=n_down),
        grid=(B, S // tm),
        in_specs=[pl.BlockSpec((1, tm, D), lambda b, t: (b, t, 0)),
                  pl.BlockSpec((None, 1, tm, p.shape[-1]), lambda b, t: (layer, b, t, 0)),
                  lspec(gffn)]
        + _column_block_specs(layer, D, n_up)
        + [lspec(cw), lspec(cb)]
        + _column_block_specs(layer, d_ff, n_down)
        + [lspec(gple)]
        + _column_block_specs(layer, D, n_down)
        + _column_block_specs(layer, p.shape[-1], n_down)
        + [_const_spec(gfin.shape)],
        out_specs=pl.BlockSpec((1, tm, D), lambda b, t: (b, t, 0)),
        out_shape=jax.ShapeDtypeStruct(x.shape, F32),
        scratch_shapes=[
            pltpu.VMEM((n_up, MXU_N // LANES, tm + TAIL, LANES), F32),
            pltpu.VMEM((tm, d_ff), BF16),
            pltpu.VMEM((tm, D), F32),
            pltpu.VMEM((tm, D), BF16),
            pltpu.VMEM((tm, D), F32),
        ],
        compiler_params=pltpu.CompilerParams(
            dimension_semantics=("arbitrary", "arbitrary"), vmem_limit_bytes=VMEM_LIMIT_BYTES),
        name="channel_mix",
    )(x, p, gffn, *([wup] * n_up), cw, cb, *([wdown] * n_down), gple, *([wpg] * n_down),
      *([wple] * n_down), gfin)


def kernel(x, p, g_mix, w_in, gm_ln_g, gm_ln_b, gm_ws, gm_bs, ml_conv_w, ml_conv_b, ml_b_i, ml_b_f,
           ml_norm_g, w_out, g_ffn, w_up, ffn_conv_w, ffn_conv_b, w_down, g_ple, w_ple_gate, w_ple,
           g_final):
    depth = g_mix.shape[0]
    seq = x.shape[1]
    tm_mix, tm_ffn = min(MIX_TILE, seq), min(FFN_TILE, seq)
    n_main = 2 * GM_WIDTH + 4 * ML_WIDTH
    n_gates = 2 * ML_HEADS

    rows = lambda a: a[:, None, :]
    win = w_in.astype(BF16)
    wgate = jnp.pad(w_in[:, :, n_main:], ((0, 0), (0, 0), (0, LANES - n_gates))).astype(BF16)
    gbias = rows(jnp.pad(jnp.concatenate([ml_b_i, ml_b_f], axis=-1), ((0, 0), (0, LANES - n_gates))))
    ws = gm_ws.reshape(depth, GM_HEADS // 2, 2, CHUNK, CHUNK).transpose(0, 1, 3, 2, 4)
    ws = ws.reshape(depth, GM_HEADS // 2, CHUNK, 2 * CHUNK)
    bs = jnp.repeat(jnp.swapaxes(gm_bs, 1, 2), GM_HEAD_DIM, axis=2)
    tri = jnp.tri(CHUNK, dtype=BF16)
    aug = jnp.zeros((CHUNK, ML_HEAD_DIM), BF16).at[:, 0].set(1)
    wout, wup, wdown = w_out.astype(BF16), w_up.astype(BF16), w_down.astype(BF16)
    wpg, wple = w_ple_gate.astype(BF16), w_ple.astype(BF16)

    for i in range(depth):
        x = _mix_layer(x, i, rows(g_mix), win, wgate, gbias, rows(gm_ln_g), rows(gm_ln_b), ws, bs,
                       ml_conv_w, rows(ml_conv_b), rows(ml_norm_g), wout, tri, aug, tm=tm_mix)
        x = _ffn_layer(x, p, i, rows(g_ffn), wup, ffn_conv_w, rows(ffn_conv_b), wdown, rows(g_ple), wpg,
                       wple, g_final.reshape(1, -1), tm=tm_ffn, final=(i == depth - 1))
    return x
```

```python
import functools

import jax
import jax.numpy as jnp
from jax import lax
from jax.experimental import pallas as pl
from jax.experimental.pallas import tpu as pltpu

EPS = 1e-6
CHUNK = 128
GM_HEADS, GM_HEAD_DIM = 8, 64
GM_WIDTH = GM_HEADS * GM_HEAD_DIM
ML_HEADS, ML_HEAD_DIM = 4, 128
ML_WIDTH = ML_HEADS * ML_HEAD_DIM
QK_CONV, FFN_CONV = 4, 3
TAIL = 8
LANES = 128
MXU_N = 256
MIX_TILE = 512
MIX_SECTIONS = 6
MIX_PIECE_GROUP = 3
FFN_TILE = 512
VMEM_LIMIT_BYTES = 56 * 1024 * 1024

F32 = jnp.float32
BF16 = jnp.bfloat16


def _dot(a, b):
    return jnp.dot(a, b, preferred_element_type=F32)


def _rms_norm(x, g):
    return x * lax.rsqrt(jnp.mean(x * x, axis=-1, keepdims=True) + EPS) * g


def _gelu(x):
    return 0.5 * x * (1.0 + lax.erf(x * (2.0 ** -0.5)))


def _split3_bf16(x):
    hi = x.astype(BF16)
    r = x - hi.astype(F32)
    mid = r.astype(BF16)
    lo = (r - mid.astype(F32)).astype(BF16)
    return hi, mid, lo


def _layer_spec(layer, shape):
    zeros = (0,) * len(shape)
    return pl.BlockSpec((None, *shape), lambda *_: (layer, *zeros), pipeline_mode=pl.Buffered(1))


def _column_block_specs(layer, rows, n_blocks, first=0):
    return [pl.BlockSpec((None, rows, MXU_N), lambda *_, n=first + n: (layer, 0, n),
                         pipeline_mode=pl.Buffered(1)) for n in range(n_blocks)]


def _const_spec(shape):
    return pl.BlockSpec(shape, lambda *_: (0,) * len(shape), pipeline_mode=pl.Buffered(1))


def _mix_kernel(*refs, tm, n_in, n_out, tiles_per_seq):
    x_ref, xlag_ref, g_ref = refs[0:3]
    win_refs = refs[3:3 + n_in]
    (wgate_ref, gbias_ref, lng_ref, lnb_ref, ws_ref, bs_ref, cw_ref, cb_ref,
     ng_ref) = refs[3 + n_in:12 + n_in]
    wout_refs = refs[12 + n_in:12 + n_in + n_out]
    tri_ref, aug_ref, o_ref = refs[12 + n_in + n_out:15 + n_in + n_out]
    scratch = refs[15 + n_in + n_out:]
    slots = (scratch[0:4], scratch[4:8])
    mixes = scratch[8:10]
    c_s, m_s, h_s = scratch[10:13]
    L, dh = CHUNK, ML_HEAD_DIM
    per = n_in // 3
    step = pl.program_id(0)

    @pl.when(step == 0)
    def _():
        for buf in (*slots[1], mixes[0], c_s, m_s):
            buf[...] = jnp.zeros(buf.shape, buf.dtype)

    @pl.when((step >= 1) & (lax.rem(step - 1, tiles_per_seq) == 0))
    def _():
        for uv_s, qk_s, vo_s, gates_s in slots:
            qk_s[:, 0:TAIL, :] = jnp.zeros((qk_s.shape[0], TAIL, LANES), F32)
        c_s[...] = jnp.zeros(c_s.shape, F32)
        m_s[...] = jnp.zeros(m_s.shape, F32)

    def project_pieces(dst):
        uv_s, qk_s, vo_s, gates_s = dst

        def norm():
            h_s[...] = _rms_norm(x_ref[0], g_ref[...]).astype(BF16)

        def block(i, n):
            def run():
                res = _dot(h_s[...], win_refs[i * per + n][...])
                cols = slice(n * MXU_N, (n + 1) * MXU_N)
                if i == 0:
                    uv_s[:, cols] = res
                elif i == 2:
                    vo_s[:, cols] = res
                else:
                    for half in range(MXU_N // LANES):
                        qk_s[n * (MXU_N // LANES) + half, TAIL:TAIL + tm, :] = res[:, half * LANES:(half + 1) * LANES]
            return run

        def gates():
            gates_s[...] = _dot(h_s[...], wgate_ref[...]) + gbias_ref[...]

        return [norm] + [block(i, n) for i in range(3) for n in range(per)] + [gates]

    def out_project_pieces(mix_s):
        def block(n):
            def run():
                cols = slice(n * MXU_N, (n + 1) * MXU_N)
                o_ref[0, :, cols] = xlag_ref[0, :, cols] + _dot(mix_s[...], wout_refs[n][...])
            return run

        return [block(n) for n in range(n_out)]

    row = lax.broadcasted_iota(jnp.int32, (L, L), 0)
    col = lax.broadcasted_iota(jnp.int32, (L, L), 1)
    causal = col <= row
    lo_half = lax.broadcasted_iota(jnp.int32, (L, LANES), 1) < GM_HEAD_DIM
    causal2 = jnp.concatenate([causal, causal], axis=1)
    k_scale = dh ** -0.5

    def mix_chunk(c, src, mix_s, ws_masked):
        uv_s, qk_s, vo_s, gates_s = src
        r0 = c * L
        u = _gelu(uv_s[r0:r0 + L, 0:GM_WIDTH])
        v = _gelu(uv_s[r0:r0 + L, GM_WIDTH:2 * GM_WIDTH])
        mu = jnp.mean(v, axis=-1, keepdims=True)
        vc = v - mu
        v = vc * lax.rsqrt(jnp.mean(vc * vc, axis=-1, keepdims=True) + EPS)
        v = v * lng_ref[...] + lnb_ref[...]
        for j in range(GM_HEADS // 2):
            vp = v[:, j * LANES:(j + 1) * LANES]
            rhs = jnp.concatenate([jnp.where(lo_half, vp, 0.0), jnp.where(lo_half, 0.0, vp)], axis=0)
            mixed = _dot(ws_masked[j], rhs.astype(BF16)) + bs_ref[:, j * LANES:(j + 1) * LANES]
            mix_s[r0:r0 + L, j * LANES:(j + 1) * LANES] = (u[:, j * LANES:(j + 1) * LANES] * mixed).astype(BF16)
        yield

        def conv_silu(slab):
            cols = slice(slab * LANES, (slab + 1) * LANES)
            acc = cb_ref[:, cols]
            for j in range(QK_CONV):
                off = r0 + TAIL - (QK_CONV - 1) + j
                acc = acc + cw_ref[j:j + 1, cols] * qk_s[slab, off:off + L, :]
            return acc * jax.nn.sigmoid(acc)

        g = gates_s[r0:r0 + L, :]
        lf = jax.nn.log_sigmoid(g)
        hi, mid, lo = _split3_bf16(lf)
        tri = tri_ref[...]
        bc = _dot(tri, hi) + _dot(tri, mid) + _dot(tri, lo)
        g_t = g.T
        bc_t = bc.T

        heads = range(ML_HEADS)
        yield
        cols, qb, kf, vb, c0b, m0s, s_raw = {}, {}, {}, {}, {}, {}, {}
        for hd in heads:
            b_col = bc[:, ML_HEADS + hd:ML_HEADS + hd + 1]
            b_row = bc_t[ML_HEADS + hd:ML_HEADS + hd + 1, :]
            b_last = b_col[L - 1:L, :]
            cols[hd] = (b_col, b_row, g_t[hd:hd + 1, :])
            qb[hd] = conv_silu(hd).astype(BF16)
            kf[hd] = conv_silu(ML_HEADS + hd) * k_scale
            vb[hd] = vo_s[r0:r0 + L, hd * dh:(hd + 1) * dh].astype(BF16)
            s_raw[hd] = lax.dot_general(qb[hd], kf[hd].astype(BF16), (((1,), (1,)), ((), ())),
                                        preferred_element_type=F32)
            m0 = m_s[hd][:, 0:1]
            c0 = c_s[hd]
            m0s[hd], c0b[hd] = m0, c0.astype(BF16)
            a_col = b_last - b_col + g[:, hd:hd + 1]
            m_new = jnp.maximum(b_last + m0, jnp.max(a_col, axis=0, keepdims=True))
            decay = jnp.exp(b_last + m0 - m_new)
            kw = (kf[hd] * jnp.exp(a_col - m_new)).astype(BF16)
            v_aug = jnp.concatenate([vb[hd], aug_ref[...]], axis=1)
            upd = lax.dot_general(kw, v_aug, (((0,), (0,)), ((), ())), preferred_element_type=F32)
            c_s[hd] = decay * c0 + upd
            m_s[hd] = jnp.broadcast_to(m_new, (1, LANES))
        yield
        w_d, inter_w, m_t = {}, {}, {}
        for hd in heads:
            b_col, b_row, ig_row = cols[hd]
            log_d = jnp.where(causal, b_col - b_row + ig_row, -jnp.inf)
            inter = b_col + m0s[hd]
            m_t[hd] = jnp.maximum(jnp.max(log_d, axis=-1, keepdims=True), inter)
            w_d[hd] = jnp.exp(log_d - m_t[hd])
            inter_w[hd] = jnp.exp(inter - m_t[hd])
        yield
        num, den = {}, {}
        for hd in heads:
            s = s_raw[hd] * w_d[hd]
            qc = _dot(qb[hd], c0b[hd])
            num[hd] = inter_w[hd] * qc[:, 0:dh] + _dot(s.astype(BF16), vb[hd])
            den[hd] = inter_w[hd] * qc[:, dh:dh + 1] + jnp.sum(s, axis=-1, keepdims=True)
        yield
        for hd in heads:
            og = vo_s[r0:r0 + L, ML_WIDTH + hd * dh:ML_WIDTH + (hd + 1) * dh]
            hr = num[hd] * (1.0 / jnp.maximum(jnp.abs(den[hd]), jnp.exp(-m_t[hd])))
            hc = jax.nn.sigmoid(og) * hr
            hn = hc * lax.rsqrt(jnp.mean(hc * hc, axis=-1, keepdims=True) + EPS)
            hn = hn * ng_ref[:, hd * dh:(hd + 1) * dh]
            mix_s[r0:r0 + L, GM_WIDTH + hd * dh:GM_WIDTH + (hd + 1) * dh] = hn.astype(BF16)
        yield

    def stages(a, b):
        ws_masked = [jnp.where(causal2, ws_ref[j], 0.0).astype(BF16) for j in range(GM_HEADS // 2)]
        norm, *pieces = project_pieces(slots[a]) + out_project_pieces(mixes[a])
        groups = [pieces[i:i + MIX_PIECE_GROUP] for i in range(0, len(pieces), MIX_PIECE_GROUP)]
        sections = [gen for c in range(tm // L)
                    for gen in [mix_chunk(c, slots[b], mixes[b], ws_masked)] * MIX_SECTIONS]
        norm()
        for si, gen in enumerate(sections):
            for j, group in enumerate(groups):
                if -(-j * len(sections) // len(groups)) == si:
                    for piece in group:
                        piece()
            next(gen)
        slots[a][1][:, 0:TAIL, :] = slots[b][1][:, tm:tm + TAIL, :]

    parity = lax.rem(step, 2)
    pl.when(parity == 0)(lambda: stages(0, 1))
    pl.when(parity == 1)(lambda: stages(1, 0))


def _mix_layer(x, layer, g, win, wgate, gbias, lng, lnb, ws, bs, cw, cb, ng, wout, tri, aug, *, tm):
    B, S, D = x.shape
    n_in = (2 * GM_WIDTH + 4 * ML_WIDTH) // MXU_N
    n_out = wout.shape[2] // MXU_N
    tiles_per_seq = S // tm
    n_tiles = B * tiles_per_seq
    lspec = lambda a: _layer_spec(layer, a.shape[1:])

    def tile_spec(lag):
        def index(s):
            tile = jnp.clip(s - lag, 0, n_tiles - 1)
            return (tile // tiles_per_seq, tile % tiles_per_seq, 0)
        return pl.BlockSpec((1, tm, D), index)

    slot = [
        pltpu.VMEM((tm, 2 * GM_WIDTH), F32),
        pltpu.VMEM((2 * ML_WIDTH // LANES, tm + TAIL, LANES), F32),
        pltpu.VMEM((tm, 2 * ML_WIDTH), F32),
        pltpu.VMEM((tm, LANES), F32),
    ]
    return pl.pallas_call(
        functools.partial(_mix_kernel, tm=tm, n_in=n_in, n_out=n_out, tiles_per_seq=tiles_per_seq),
        grid=(n_tiles + 2,),
        in_specs=[tile_spec(0), tile_spec(2), lspec(g)]
        + _column_block_specs(layer, D, n_in)
        + [lspec(a) for a in (wgate, gbias, lng, lnb, ws, bs, cw, cb, ng)]
        + _column_block_specs(layer, wout.shape[1], n_out)
        + [_const_spec(tri.shape), _const_spec(aug.shape)],
        out_specs=tile_spec(2),
        out_shape=jax.ShapeDtypeStruct(x.shape, F32),
        scratch_shapes=slot + slot + [
            pltpu.VMEM((tm, GM_WIDTH + ML_WIDTH), BF16),
            pltpu.VMEM((tm, GM_WIDTH + ML_WIDTH), BF16),
            pltpu.VMEM((ML_HEADS, ML_HEAD_DIM, 2 * ML_HEAD_DIM), F32),
            pltpu.VMEM((ML_HEADS, 1, LANES), F32),
            pltpu.VMEM((tm, D), BF16),
        ],
        compiler_params=pltpu.CompilerParams(
            dimension_semantics=("arbitrary",), vmem_limit_bytes=VMEM_LIMIT_BYTES),
        name="token_mix",
    )(x, x, g, *([win] * n_in), wgate, gbias, lng, lnb, ws, bs, cw, cb, ng, *([wout] * n_out), tri, aug)


def _ffn_kernel(*refs, tm, final, n_up, n_down):
    x_ref, p_ref, gffn_ref = refs[0:3]
    wup_refs = refs[3:3 + n_up]
    cw_ref, cb_ref = refs[3 + n_up:5 + n_up]
    wdown_refs = refs[5 + n_up:5 + n_up + n_down]
    gple_ref = refs[5 + n_up + n_down]
    wpg_refs = refs[6 + n_up + n_down:6 + n_up + 2 * n_down]
    wple_refs = refs[6 + n_up + 2 * n_down:6 + n_up + 3 * n_down]
    gfin_ref, o_ref = refs[6 + n_up + 3 * n_down:8 + n_up + 3 * n_down]
    ubuf_s, act_s, x1_s, h2_s, pe_s = refs[8 + n_up + 3 * n_down:]
    nblk = n_up // 2
    slabs = MXU_N // LANES

    @pl.when(pl.program_id(1) == 0)
    def _():
        ubuf_s[:, :, 0:TAIL, :] = jnp.zeros((ubuf_s.shape[0], ubuf_s.shape[1], TAIL, LANES), F32)

    pb = p_ref[0].astype(BF16)
    for n in range(n_down):
        cols = slice(n * MXU_N, (n + 1) * MXU_N)
        pe_s[:, cols] = _dot(pb, wple_refs[n][...])
    h = _rms_norm(x_ref[0], gffn_ref[...]).astype(BF16)
    for j in range(nblk):
        halves = []
        for part in range(2):
            idx = j + part * nblk
            up = _dot(h, wup_refs[idx][...])
            cv = []
            for sl in range(slabs):
                cols = slice(idx * MXU_N + sl * LANES, idx * MXU_N + (sl + 1) * LANES)
                ubuf_s[idx, sl, TAIL:TAIL + tm, :] = up[:, sl * LANES:(sl + 1) * LANES]
                acc = cb_ref[:, cols]
                for k in range(FFN_CONV):
                    off = TAIL - (FFN_CONV - 1) + k
                    acc = acc + cw_ref[k:k + 1, cols] * ubuf_s[idx, sl, off:off + tm, :]
                ubuf_s[idx, sl, 0:TAIL, :] = ubuf_s[idx, sl, tm:tm + TAIL, :]
                cv.append(acc)
            halves.append(jnp.concatenate(cv, axis=1))
        a, bv = halves
        act_s[:, j * MXU_N:(j + 1) * MXU_N] = (a * jax.nn.sigmoid(a) * bv).astype(BF16)

    for n in range(n_down):
        cols = slice(n * MXU_N, (n + 1) * MXU_N)
        x1_s[:, cols] = x_ref[0, :, cols] + _dot(act_s[...], wdown_refs[n][...])

    h2_s[...] = _rms_norm(x1_s[...], gple_ref[...]).astype(BF16)
    for n in range(n_down):
        cols = slice(n * MXU_N, (n + 1) * MXU_N)
        gate = jax.nn.sigmoid(_dot(h2_s[...], wpg_refs[n][...]))
        o_ref[0, :, cols] = x1_s[:, cols] + gate * pe_s[:, cols]
    if final:
        o_ref[0] = _rms_norm(o_ref[0], gfin_ref[...])


def _ffn_layer(x, p, layer, gffn, wup, cw, cb, wdown, gple, wpg, wple, gfin, *, tm, final):
    B, S, D = x.shape
    d_ff = wdown.shape[1]
    n_up = 2 * d_ff // MXU_N
    n_down = D // MXU_N
    lspec = lambda a: _layer_spec(layer, a.shape[1:])
    return pl.pallas_call(
        functools.partial(_ffn_kernel, tm=tm, final=final, n_up=n_up, n_down=n_down),
        grid=(B, S // tm),
        in_specs=[pl.BlockSpec((1, tm, D), lambda b, t: (b, t, 0)),
                  pl.BlockSpec((None, 1, tm, p.shape[-1]), lambda b, t: (layer, b, t, 0)),
                  lspec(gffn)]
        + _column_block_specs(layer, D, n_up)
        + [lspec(cw), lspec(cb)]
        + _column_block_specs(layer, d_ff, n_down)
        + [lspec(gple)]
        + _column_block_specs(layer, D, n_down)
        + _column_block_specs(layer, p.shape[-1], n_down)
        + [_const_spec(gfin.shape)],
        out_specs=pl.BlockSpec((1, tm, D), lambda b, t: (b, t, 0)),
        out_shape=jax.ShapeDtypeStruct(x.shape, F32),
        scratch_shapes=[
            pltpu.VMEM((n_up, MXU_N // LANES, tm + TAIL, LANES), F32),
            pltpu.VMEM((tm, d_ff), BF16),
            pltpu.VMEM((tm, D), F32),
            pltpu.VMEM((tm, D), BF16),
            pltpu.VMEM((tm, D), F32),
        ],
        compiler_params=pltpu.CompilerParams(
            dimension_semantics=("arbitrary", "arbitrary"), vmem_limit_bytes=VMEM_LIMIT_BYTES),
        name="channel_mix",
    )(x, p, gffn, *([wup] * n_up), cw, cb, *([wdown] * n_down), gple, *([wpg] * n_down),
      *([wple] * n_down), gfin)


def kernel(x, p, g_mix, w_in, gm_ln_g, gm_ln_b, gm_ws, gm_bs, ml_conv_w, ml_conv_b, ml_b_i, ml_b_f,
           ml_norm_g, w_out, g_ffn, w_up, ffn_conv_w, ffn_conv_b, w_down, g_ple, w_ple_gate, w_ple,
           g_final):
    depth = g_mix.shape[0]
    seq = x.shape[1]
    tm_mix, tm_ffn = min(MIX_TILE, seq), min(FFN_TILE, seq)
    n_main = 2 * GM_WIDTH + 4 * ML_WIDTH
    n_gates = 2 * ML_HEADS

    rows = lambda a: a[:, None, :]
    win = w_in.astype(BF16)
    wgate = jnp.pad(w_in[:, :, n_main:], ((0, 0), (0, 0), (0, LANES - n_gates))).astype(BF16)
    gbias = rows(jnp.pad(jnp.concatenate([ml_b_i, ml_b_f], axis=-1), ((0, 0), (0, LANES - n_gates))))
    ws = gm_ws.reshape(depth, GM_HEADS // 2, 2, CHUNK, CHUNK).transpose(0, 1, 3, 2, 4)
    ws = ws.reshape(depth, GM_HEADS // 2, CHUNK, 2 * CHUNK)
    bs = jnp.repeat(jnp.swapaxes(gm_bs, 1, 2), GM_HEAD_DIM, axis=2)
    tri = jnp.tri(CHUNK, dtype=BF16)
    aug = jnp.zeros((CHUNK, ML_HEAD_DIM), BF16).at[:, 0].set(1)
    wout, wup, wdown = w_out.astype(BF16), w_up.astype(BF16), w_down.astype(BF16)
    wpg, wple = w_ple_gate.astype(BF16), w_ple.astype(BF16)

    for i in range(depth):
        x = _mix_layer(x, i, rows(g_mix), win, wgate, gbias, rows(gm_ln_g), rows(gm_ln_b), ws, bs,
                       ml_conv_w, rows(ml_conv_b), rows(ml_norm_g), wout, tri, aug, tm=tm_mix)
        x = _ffn_layer(x, p, i, rows(g_ffn), wup, ffn_conv_w, rows(ffn_conv_b), wdown, rows(g_ple), wpg,
                       wple, g_final.reshape(1, -1), tm=tm_ffn, final=(i == depth - 1))
    return x
```

```python
import functools

import jax
import jax.numpy as jnp
from jax import lax
from jax.experimental import pallas as pl
from jax.experimental.pallas import tpu as pltpu

EPS = 1e-6
CHUNK = 128
GM_HEADS, GM_HEAD_DIM = 8, 64
GM_WIDTH = GM_HEADS * GM_HEAD_DIM
ML_HEADS, ML_HEAD_DIM = 4, 128
ML_WIDTH = ML_HEADS * ML_HEAD_DIM
QK_CONV, FFN_CONV = 4, 3
TAIL = 8
LANES = 128
MXU_N = 256
MIX_TILE = 512
MIX_SECTIONS = 6
MIX_PIECE_GROUP = 3
FFN_TILE = 512
VMEM_LIMIT_BYTES = 56 * 1024 * 1024

F32 = jnp.float32
BF16 = jnp.bfloat16


def _dot(a, b):
    return jnp.dot(a, b, preferred_element_type=F32)


def _rms_norm(x, g):
    return x * lax.rsqrt(jnp.mean(x * x, axis=-1, keepdims=True) + EPS) * g


def _gelu(x):
    return 0.5 * x * (1.0 + lax.erf(x * (2.0 ** -0.5)))


def _split3_bf16(x):
    hi = x.astype(BF16)
    r = x - hi.astype(F32)
    mid = r.astype(BF16)
    lo = (r - mid.astype(F32)).astype(BF16)
    return hi, mid, lo


def _layer_spec(layer, shape):
    zeros = (0,) * len(shape)
    return pl.BlockSpec((None, *shape), lambda *_: (layer, *zeros), pipeline_mode=pl.Buffered(1))


def _column_block_specs(layer, rows, n_blocks, first=0):
    return [pl.BlockSpec((None, rows, MXU_N), lambda *_, n=first + n: (layer, 0, n),
                         pipeline_mode=pl.Buffered(1)) for n in range(n_blocks)]


def _const_spec(shape):
    return pl.BlockSpec(shape, lambda *_: (0,) * len(shape), pipeline_mode=pl.Buffered(1))


def _mix_kernel(*refs, tm, n_in, n_out, tiles_per_seq):
    x_ref, xlag_ref, g_ref = refs[0:3]
    win_refs = refs[3:3 + n_in]
    (wgate_ref, gbias_ref, lng_ref, lnb_ref, ws_ref, bs_ref, cw_ref, cb_ref,
     ng_ref) = refs[3 + n_in:12 + n_in]
    wout_refs = refs[12 + n_in:12 + n_in + n_out]
    tri_ref, aug_ref, o_ref = refs[12 + n_in + n_out:15 + n_in + n_out]
    scratch = refs[15 + n_in + n_out:]
    slots = (scratch[0:4], scratch[4:8])
    mixes = scratch[8:10]
    c_s, m_s, h_s = scratch[10:13]
    L, dh = CHUNK, ML_HEAD_DIM
    per = n_in // 3
    step = pl.program_id(0)

    @pl.when(step == 0)
    def _():
        for buf in (*slots[1], mixes[0], c_s, m_s):
            buf[...] = jnp.zeros(buf.shape, buf.dtype)

    @pl.when((step >= 1) & (lax.rem(step - 1, tiles_per_seq) == 0))
    def _():
        for uv_s, qk_s, vo_s, gates_s in slots:
            qk_s[:, 0:TAIL, :] = jnp.zeros((qk_s.shape[0], TAIL, LANES), F32)
        c_s[...] = jnp.zeros(c_s.shape, F32)
        m_s[...] = jnp.zeros(m_s.shape, F32)

    def project_pieces(dst):
        uv_s, qk_s, vo_s, gates_s = dst

        def norm():
            h_s[...] = _rms_norm(x_ref[0], g_ref[...]).astype(BF16)

        def block(i, n):
            def run():
                res = _dot(h_s[...], win_refs[i * per + n][...])
                cols = slice(n * MXU_N, (n + 1) * MXU_N)
                if i == 0:
                    uv_s[:, cols] = res
                elif i == 2:
                    vo_s[:, cols] = res
                else:
                    for half in range(MXU_N // LANES):
                        qk_s[n * (MXU_N // LANES) + half, TAIL:TAIL + tm, :] = res[:, half * LANES:(half + 1) * LANES]
            return run

        def gates():
            gates_s[...] = _dot(h_s[...], wgate_ref[...]) + gbias_ref[...]

        return [norm] + [block(i, n) for i in range(3) for n in range(per)] + [gates]

    def out_project_pieces(mix_s):
        def block(n):
            def run():
                cols = slice(n * MXU_N, (n + 1) * MXU_N)
                o_ref[0, :, cols] = xlag_ref[0, :, cols] + _dot(mix_s[...], wout_refs[n][...])
            return run

        return [block(n) for n in range(n_out)]

    row = lax.broadcasted_iota(jnp.int32, (L, L), 0)
    col = lax.broadcasted_iota(jnp.int32, (L, L), 1)
    causal = col <= row
    lo_half = lax.broadcasted_iota(jnp.int32, (L, LANES), 1) < GM_HEAD_DIM
    causal2 = jnp.concatenate([causal, causal], axis=1)
    k_scale = dh ** -0.5

    def mix_chunk(c, src, mix_s, ws_masked):
        uv_s, qk_s, vo_s, gates_s = src
        r0 = c * L
        u = _gelu(uv_s[r0:r0 + L, 0:GM_WIDTH])
        v = _gelu(uv_s[r0:r0 + L, GM_WIDTH:2 * GM_WIDTH])
        mu = jnp.mean(v, axis=-1, keepdims=True)
        vc = v - mu
        v = vc * lax.rsqrt(jnp.mean(vc * vc, axis=-1, keepdims=True) + EPS)
        v = v * lng_ref[...] + lnb_ref[...]
        for j in range(GM_HEADS // 2):
            vp = v[:, j * LANES:(j + 1) * LANES]
            rhs = jnp.concatenate([jnp.where(lo_half, vp, 0.0), jnp.where(lo_half, 0.0, vp)], axis=0)
            mixed = _dot(ws_masked[j], rhs.astype(BF16)) + bs_ref[:, j * LANES:(j + 1) * LANES]
            mix_s[r0:r0 + L, j * LANES:(j + 1) * LANES] = (u[:, j * LANES:(j + 1) * LANES] * mixed).astype(BF16)
        yield

        def conv_silu(slab):
            cols = slice(slab * LANES, (slab + 1) * LANES)
            acc = cb_ref[:, cols]
            for j in range(QK_CONV):
                off = r0 + TAIL - (QK_CONV - 1) + j
                acc = acc + cw_ref[j:j + 1, cols] * qk_s[slab, off:off + L, :]
            return acc * jax.nn.sigmoid(acc)

        g = gates_s[r0:r0 + L, :]
        lf = jax.nn.log_sigmoid(g)
        hi, mid, lo = _split3_bf16(lf)
        tri = tri_ref[...]
        bc = _dot(tri, hi) + _dot(tri, mid) + _dot(tri, lo)
        g_t = g.T
        bc_t = bc.T

        heads = range(ML_HEADS)
        yield
        cols, qb, kf, vb, c0b, m0s, s_raw = {}, {}, {}, {}, {}, {}, {}
        for hd in heads:
            b_col = bc[:, ML_HEADS + hd:ML_HEADS + hd + 1]
            b_row = bc_t[ML_HEADS + hd:ML_HEADS + hd + 1, :]
            b_last = b_col[L - 1:L, :]
            cols[hd] = (b_col, b_row, g_t[hd:hd + 1, :])
            qb[hd] = conv_silu(hd).astype(BF16)
            kf[hd] = conv_silu(ML_HEADS + hd) * k_scale
            vb[hd] = vo_s[r0:r0 + L, hd * dh:(hd + 1) * dh].astype(BF16)
            s_raw[hd] = lax.dot_general(qb[hd], kf[hd].astype(BF16), (((1,), (1,)), ((), ())),
                                        preferred_element_type=F32)
            m0 = m_s[hd][:, 0:1]
            c0 = c_s[hd]
            m0s[hd], c0b[hd] = m0, c0.astype(BF16)
            a_col = b_last - b_col + g[:, hd:hd + 1]
            m_new = jnp.maximum(b_last + m0, jnp.max(a_col, axis=0, keepdims=True))
            decay = jnp.exp(b_last + m0 - m_new)
            kw = (kf[hd] * jnp.exp(a_col - m_new)).astype(BF16)
            v_aug = jnp.concatenate([vb[hd], aug_ref[...]], axis=1)
            upd = lax.dot_general(kw, v_aug, (((0,), (0,)), ((), ())), preferred_element_type=F32)
            c_s[hd] = decay * c0 + upd
            m_s[hd] = jnp.broadcast_to(m_new, (1, LANES))
        yield
        w_d, inter_w, m_t = {}, {}, {}
        for hd in heads:
            b_col, b_row, ig_row = cols[hd]
            beta = ig_row - b_row
            alpha = -jnp.maximum(jnp.max(jnp.where(causal, beta, -jnp.inf), axis=-1, keepdims=True), m0s[hd])
            m_t[hd] = b_col - alpha
            w_d[hd] = jnp.exp(jnp.where(causal, alpha + beta, -jnp.inf))
            inter_w[hd] = jnp.exp(alpha + m0s[hd])
        yield
        num, den = {}, {}
        for hd in heads:
            s = s_raw[hd] * w_d[hd]
            qc = _dot(qb[hd], c0b[hd])
            num[hd] = inter_w[hd] * qc[:, 0:dh] + _dot(s.astype(BF16), vb[hd])
            den[hd] = inter_w[hd] * qc[:, dh:dh + 1] + jnp.sum(s, axis=-1, keepdims=True)
        yield
        for hd in heads:
            og = vo_s[r0:r0 + L, ML_WIDTH + hd * dh:ML_WIDTH + (hd + 1) * dh]
            hr = num[hd] * (1.0 / jnp.maximum(jnp.abs(den[hd]), jnp.exp(-m_t[hd])))
            hc = jax.nn.sigmoid(og) * hr
            hn = hc * lax.rsqrt(jnp.mean(hc * hc, axis=-1, keepdims=True) + EPS)
            hn = hn * ng_ref[:, hd * dh:(hd + 1) * dh]
            mix_s[r0:r0 + L, GM_WIDTH + hd * dh:GM_WIDTH + (hd + 1) * dh] = hn.astype(BF16)
        yield

    def stages(a, b):
        ws_masked = [jnp.where(causal2, ws_ref[j], 0.0).astype(BF16) for j in range(GM_HEADS // 2)]
        norm, *pieces = project_pieces(slots[a]) + out_project_pieces(mixes[a])
        groups = [pieces[i:i + MIX_PIECE_GROUP] for i in range(0, len(pieces), MIX_PIECE_GROUP)]
        sections = [gen for c in range(tm // L)
                    for gen in [mix_chunk(c, slots[b], mixes[b], ws_masked)] * MIX_SECTIONS]
        norm()
        for si, gen in enumerate(sections):
            for j, group in enumerate(groups):
                if -(-j * len(sections) // len(groups)) == si:
                    for piece in group:
                        piece()
            next(gen)
        assert all(next(gen, None) is None for gen in sections), "MIX_SECTIONS does not match mix_chunk's sections"
        slots[a][1][:, 0:TAIL, :] = slots[b][1][:, tm:tm + TAIL, :]

    parity = lax.rem(step, 2)
    pl.when(parity == 0)(lambda: stages(0, 1))
    pl.when(parity == 1)(lambda: stages(1, 0))


def _mix_layer(x, layer, g, win, wgate, gbias, lng, lnb, ws, bs, cw, cb, ng, wout, tri, aug, *, tm):
    B, S, D = x.shape
    n_in = (2 * GM_WIDTH + 4 * ML_WIDTH) // MXU_N
    n_out = wout.shape[2] // MXU_N
    tiles_per_seq = S // tm
    n_tiles = B * tiles_per_seq
    lspec = lambda a: _layer_spec(layer, a.shape[1:])

    def tile_spec(lag):
        def index(s):
            tile = jnp.clip(s - lag, 0, n_tiles - 1)
            return (tile // tiles_per_seq, tile % tiles_per_seq, 0)
        return pl.BlockSpec((1, tm, D), index)

    slot = [
        pltpu.VMEM((tm, 2 * GM_WIDTH), F32),
        pltpu.VMEM((2 * ML_WIDTH // LANES, tm + TAIL, LANES), F32),
        pltpu.VMEM((tm, 2 * ML_WIDTH), F32),
        pltpu.VMEM((tm, LANES), F32),
    ]
    return pl.pallas_call(
        functools.partial(_mix_kernel, tm=tm, n_in=n_in, n_out=n_out, tiles_per_seq=tiles_per_seq),
        grid=(n_tiles + 2,),
        in_specs=[tile_spec(0), tile_spec(2), lspec(g)]
        + _column_block_specs(layer, D, n_in)
        + [lspec(a) for a in (wgate, gbias, lng, lnb, ws, bs, cw, cb, ng)]
        + _column_block_specs(layer, wout.shape[1], n_out)
        + [_const_spec(tri.shape), _const_spec(aug.shape)],
        out_specs=tile_spec(2),
        out_shape=jax.ShapeDtypeStruct(x.shape, F32),
        scratch_shapes=slot + slot + [
            pltpu.VMEM((tm, GM_WIDTH + ML_WIDTH), BF16),
            pltpu.VMEM((tm, GM_WIDTH + ML_WIDTH), BF16),
            pltpu.VMEM((ML_HEADS, ML_HEAD_DIM, 2 * ML_HEAD_DIM), F32),
            pltpu.VMEM((ML_HEADS, 1, LANES), F32),
            pltpu.VMEM((tm, D), BF16),
        ],
        compiler_params=pltpu.CompilerParams(
            dimension_semantics=("arbitrary",), vmem_limit_bytes=VMEM_LIMIT_BYTES),
        name="token_mix",
    )(x, x, g, *([win] * n_in), wgate, gbias, lng, lnb, ws, bs, cw, cb, ng, *([wout] * n_out), tri, aug)


def _ffn_kernel(*refs, tm, final, n_up, n_down):
    x_ref, p_ref, gffn_ref = refs[0:3]
    wup_refs = refs[3:3 + n_up]
    cw_ref, cb_ref = refs[3 + n_up:5 + n_up]
    wdown_refs = refs[5 + n_up:5 + n_up + n_down]
    gple_ref = refs[5 + n_up + n_down]
    wpg_refs = refs[6 + n_up + n_down:6 + n_up + 2 * n_down]
    wple_refs = refs[6 + n_up + 2 * n_down:6 + n_up + 3 * n_down]
    gfin_ref, o_ref = refs[6 + n_up + 3 * n_down:8 + n_up + 3 * n_down]
    ubuf_s, act_s, x1_s, h2_s, pe_s = refs[8 + n_up + 3 * n_down:]
    nblk = n_up // 2
    slabs = MXU_N // LANES

    @pl.when(pl.program_id(1) == 0)
    def _():
        ubuf_s[:, :, 0:TAIL, :] = jnp.zeros((ubuf_s.shape[0], ubuf_s.shape[1], TAIL, LANES), F32)

    pb = p_ref[0].astype(BF16)
    for n in range(n_down):
        cols = slice(n * MXU_N, (n + 1) * MXU_N)
        pe_s[:, cols] = _dot(pb, wple_refs[n][...])
    h = _rms_norm(x_ref[0], gffn_ref[...]).astype(BF16)
    for j in range(nblk):
        halves = []
        for part in range(2):
            idx = j + part * nblk
            up = _dot(h, wup_refs[idx][...])
            cv = []
            for sl in range(slabs):
                cols = slice(idx * MXU_N + sl * LANES, idx * MXU_N + (sl + 1) * LANES)
                ubuf_s[idx, sl, TAIL:TAIL + tm, :] = up[:, sl * LANES:(sl + 1) * LANES]
                acc = cb_ref[:, cols]
                for k in range(FFN_CONV):
                    off = TAIL - (FFN_CONV - 1) + k
                    acc = acc + cw_ref[k:k + 1, cols] * ubuf_s[idx, sl, off:off + tm, :]
                ubuf_s[idx, sl, 0:TAIL, :] = ubuf_s[idx, sl, tm:tm + TAIL, :]
                cv.append(acc)
            halves.append(jnp.concatenate(cv, axis=1))
        a, bv = halves
        act_s[:, j * MXU_N:(j + 1) * MXU_N] = (a * jax.nn.sigmoid(a) * bv).astype(BF16)

    for n in range(n_down):
        cols = slice(n * MXU_N, (n + 1) * MXU_N)
        x1_s[:, cols] = x_ref[0, :, cols] + _dot(act_s[...], wdown_refs[n][...])

    h2_s[...] = _rms_norm(x1_s[...], gple_ref[...]).astype(BF16)
    for n in range(n_down):
        cols = slice(n * MXU_N, (n + 1) * MXU_N)
        gate = jax.nn.sigmoid(_dot(h2_s[...], wpg_refs[n][...]))
        o_ref[0, :, cols] = x1_s[:, cols] + gate * pe_s[:, cols]
    if final:
        o_ref[0] = _rms_norm(o_ref[0], gfin_ref[...])


def _ffn_layer(x, p, layer, gffn, wup, cw, cb, wdown, gple, wpg, wple, gfin, *, tm, final):
    B, S, D = x.shape
    d_ff = wdown.shape[1]
    n_up = 2 * d_ff // MXU_N
    n_down = D // MXU_N
    lspec = lambda a: _layer_spec(layer, a.shape[1:])
    return pl.pallas_call(
        functools.partial(_ffn_kernel, tm=tm, final=final, n_up=n_up, n_down=n_down),
        grid=(B, S // tm),
        in_specs=[pl.BlockSpec((1, tm, D), lambda b, t: (b, t, 0)),
                  pl.BlockSpec((None, 1, tm, p.shape[-1]), lambda b, t: (layer, b, t, 0)),
                  lspec(gffn)]
        + _column_block_specs(layer, D, n_up)
        + [lspec(cw), lspec(cb)]
        + _column_block_specs(layer, d_ff, n_down)
        + [lspec(gple)]
        + _column_block_specs(layer, D, n_down)
        + _column_block_specs(layer, p.shape[-1], n_down)
        + [_const_spec(gfin.shape)],
        out_specs=pl.BlockSpec((1, tm, D), lambda b, t: (b, t, 0)),
        out_shape=jax.ShapeDtypeStruct(x.shape, F32),
        scratch_shapes=[
            pltpu.VMEM((n_up, MXU_N // LANES, tm + TAIL, LANES), F32),
            pltpu.VMEM((tm, d_ff), BF16),
            pltpu.VMEM((tm, D), F32),
            pltpu.VMEM((tm, D), BF16),
            pltpu.VMEM((tm, D), F32),
        ],
        compiler_params=pltpu.CompilerParams(
            dimension_semantics=("arbitrary", "arbitrary"), vmem_limit_bytes=VMEM_LIMIT_BYTES),
        name="channel_mix",
    )(x, p, gffn, *([wup] * n_up), cw, cb, *([wdown] * n_down), gple, *([wpg] * n_down),
      *([wple] * n_down), gfin)


def kernel(x, p, g_mix, w_in, gm_ln_g, gm_ln_b, gm_ws, gm_bs, ml_conv_w, ml_conv_b, ml_b_i, ml_b_f,
           ml_norm_g, w_out, g_ffn, w_up, ffn_conv_w, ffn_conv_b, w_down, g_ple, w_ple_gate, w_ple,
           g_final):
    depth = g_mix.shape[0]
    seq = x.shape[1]
    tm_mix, tm_ffn = min(MIX_TILE, seq), min(FFN_TILE, seq)
    n_main = 2 * GM_WIDTH + 4 * ML_WIDTH
    n_gates = 2 * ML_HEADS

    rows = lambda a: a[:, None, :]
    win = w_in.astype(BF16)
    wgate = jnp.pad(w_in[:, :, n_main:], ((0, 0), (0, 0), (0, LANES - n_gates))).astype(BF16)
    gbias = rows(jnp.pad(jnp.concatenate([ml_b_i, ml_b_f], axis=-1), ((0, 0), (0, LANES - n_gates))))
    ws = gm_ws.reshape(depth, GM_HEADS // 2, 2, CHUNK, CHUNK).transpose(0, 1, 3, 2, 4)
    ws = ws.reshape(depth, GM_HEADS // 2, CHUNK, 2 * CHUNK)
    bs = jnp.repeat(jnp.swapaxes(gm_bs, 1, 2), GM_HEAD_DIM, axis=2)
    tri = jnp.tri(CHUNK, dtype=BF16)
    aug = jnp.zeros((CHUNK, ML_HEAD_DIM), BF16).at[:, 0].set(1)
    wout, wup, wdown = w_out.astype(BF16), w_up.astype(BF16), w_down.astype(BF16)
    wpg, wple = w_ple_gate.astype(BF16), w_ple.astype(BF16)

    for i in range(depth):
        x = _mix_layer(x, i, rows(g_mix), win, wgate, gbias, rows(gm_ln_g), rows(gm_ln_b), ws, bs,
                       ml_conv_w, rows(ml_conv_b), rows(ml_norm_g), wout, tri, aug, tm=tm_mix)
        x = _ffn_layer(x, p, i, rows(g_ffn), wup, ffn_conv_w, rows(ffn_conv_b), wdown, rows(g_ple), wpg,
                       wple, g_final.reshape(1, -1), tm=tm_ffn, final=(i == depth - 1))
    return x
```

```python
import functools

import jax
import jax.numpy as jnp
from jax import lax
from jax.experimental import pallas as pl
from jax.experimental.pallas import tpu as pltpu

EPS = 1e-6
CHUNK = 128
GM_HEADS, GM_HEAD_DIM = 8, 64
GM_WIDTH = GM_HEADS * GM_HEAD_DIM
ML_HEADS, ML_HEAD_DIM = 4, 128
ML_WIDTH = ML_HEADS * ML_HEAD_DIM
QK_CONV, FFN_CONV = 4, 3
TAIL = 8
LANES = 128
MXU_N = 256
MIX_TILE = 512
MIX_SECTIONS = 6
MIX_PIECE_GROUP = 3
FFN_TILE = 512
VMEM_LIMIT_BYTES = 56 * 1024 * 1024

F32 = jnp.float32
BF16 = jnp.bfloat16


def _dot(a, b):
    return jnp.dot(a, b, preferred_element_type=F32)


def _rms_norm(x, g):
    return x * lax.rsqrt(jnp.mean(x * x, axis=-1, keepdims=True) + EPS) * g


def _gelu(x):
    return 0.5 * x * (1.0 + lax.erf(x * (2.0 ** -0.5)))


def _split3_bf16(x):
    hi = x.astype(BF16)
    r = x - hi.astype(F32)
    mid = r.astype(BF16)
    lo = (r - mid.astype(F32)).astype(BF16)
    return hi, mid, lo


def _layer_spec(layer, shape):
    zeros = (0,) * len(shape)
    return pl.BlockSpec((None, *shape), lambda *_: (layer, *zeros), pipeline_mode=pl.Buffered(1))


def _column_block_specs(layer, rows, n_blocks, first=0):
    return [pl.BlockSpec((None, rows, MXU_N), lambda *_, n=first + n: (layer, 0, n),
                         pipeline_mode=pl.Buffered(1)) for n in range(n_blocks)]


def _const_spec(shape):
    return pl.BlockSpec(shape, lambda *_: (0,) * len(shape), pipeline_mode=pl.Buffered(1))


def _mix_kernel(*refs, tm, n_in, n_out, tiles_per_seq):
    x_ref, xlag_ref, g_ref = refs[0:3]
    win_refs = refs[3:3 + n_in]
    (wgate_ref, gbias_ref, lng_ref, lnb_ref, ws_ref, bs_ref, cw_ref, cb_ref,
     ng_ref) = refs[3 + n_in:12 + n_in]
    wout_refs = refs[12 + n_in:12 + n_in + n_out]
    tri_ref, aug_ref, o_ref = refs[12 + n_in + n_out:15 + n_in + n_out]
    scratch = refs[15 + n_in + n_out:]
    slots = (scratch[0:4], scratch[4:8])
    mixes = scratch[8:10]
    c_s, m_s, h_s = scratch[10:13]
    L, dh = CHUNK, ML_HEAD_DIM
    per = n_in // 3
    step = pl.program_id(0)

    @pl.when(step == 0)
    def _():
        for buf in (*slots[1], mixes[0], c_s, m_s):
            buf[...] = jnp.zeros(buf.shape, buf.dtype)

    @pl.when((step >= 1) & (lax.rem(step - 1, tiles_per_seq) == 0))
    def _():
        for uv_s, qk_s, vo_s, gates_s in slots:
            qk_s[:, 0:TAIL, :] = jnp.zeros((qk_s.shape[0], TAIL, LANES), F32)
        c_s[...] = jnp.zeros(c_s.shape, F32)
        m_s[...] = jnp.zeros(m_s.shape, F32)

    def project_pieces(dst):
        uv_s, qk_s, vo_s, gates_s = dst

        def norm():
            h_s[...] = _rms_norm(x_ref[0], g_ref[...]).astype(BF16)

        def block(i, n):
            def run():
                res = _dot(h_s[...], win_refs[i * per + n][...])
                cols = slice(n * MXU_N, (n + 1) * MXU_N)
                if i == 0:
                    uv_s[:, cols] = res
                elif i == 2:
                    vo_s[:, cols] = res
                else:
                    for half in range(MXU_N // LANES):
                        qk_s[n * (MXU_N // LANES) + half, TAIL:TAIL + tm, :] = res[:, half * LANES:(half + 1) * LANES]
            return run

        def gates():
            gates_s[...] = _dot(h_s[...], wgate_ref[...]) + gbias_ref[...]

        return [norm] + [block(i, n) for i in range(3) for n in range(per)] + [gates]

    def out_project_pieces(mix_s):
        def block(n):
            def run():
                cols = slice(n * MXU_N, (n + 1) * MXU_N)
                o_ref[0, :, cols] = xlag_ref[0, :, cols] + _dot(mix_s[...], wout_refs[n][...])
            return run

        return [block(n) for n in range(n_out)]

    row = lax.broadcasted_iota(jnp.int32, (L, L), 0)
    col = lax.broadcasted_iota(jnp.int32, (L, L), 1)
    causal = col <= row
    lo_half = lax.broadcasted_iota(jnp.int32, (L, LANES), 1) < GM_HEAD_DIM
    causal2 = jnp.concatenate([causal, causal], axis=1)
    k_scale = dh ** -0.5

    def mix_chunk(c, src, mix_s, ws_masked):
        uv_s, qk_s, vo_s, gates_s = src
        r0 = c * L
        def conv_silu(slab):
            cols = slice(slab * LANES, (slab + 1) * LANES)
            acc = cb_ref[:, cols]
            for j in range(QK_CONV):
                off = r0 + TAIL - (QK_CONV - 1) + j
                acc = acc + cw_ref[j:j + 1, cols] * qk_s[slab, off:off + L, :]
            return acc * jax.nn.sigmoid(acc)

        g = gates_s[r0:r0 + L, :]
        lf = jax.nn.log_sigmoid(g)
        hi, mid, lo = _split3_bf16(lf)
        tri = tri_ref[...]
        bc = _dot(tri, hi) + _dot(tri, mid) + _dot(tri, lo)
        g_t = g.T
        bc_t = bc.T

        heads = range(ML_HEADS)
        yield
        u = _gelu(uv_s[r0:r0 + L, 0:GM_WIDTH])
        v = _gelu(uv_s[r0:r0 + L, GM_WIDTH:2 * GM_WIDTH])
        mu = jnp.mean(v, axis=-1, keepdims=True)
        vc = v - mu
        v = vc * lax.rsqrt(jnp.mean(vc * vc, axis=-1, keepdims=True) + EPS)
        v = v * lng_ref[...] + lnb_ref[...]
        for j in range(GM_HEADS // 2):
            vp = v[:, j * LANES:(j + 1) * LANES]
            rhs = jnp.concatenate([jnp.where(lo_half, vp, 0.0), jnp.where(lo_half, 0.0, vp)], axis=0)
            mixed = _dot(ws_masked[j], rhs.astype(BF16)) + bs_ref[:, j * LANES:(j + 1) * LANES]
            mix_s[r0:r0 + L, j * LANES:(j + 1) * LANES] = (u[:, j * LANES:(j + 1) * LANES] * mixed).astype(BF16)
        yield
        cols, qb, kf, vb, c0b, m0s, s_raw = {}, {}, {}, {}, {}, {}, {}
        for hd in heads:
            b_col = bc[:, ML_HEADS + hd:ML_HEADS + hd + 1]
            b_row = bc_t[ML_HEADS + hd:ML_HEADS + hd + 1, :]
            b_last = b_col[L - 1:L, :]
            cols[hd] = (b_col, b_row, g_t[hd:hd + 1, :])
            qb[hd] = conv_silu(hd).astype(BF16)
            kf[hd] = conv_silu(ML_HEADS + hd) * k_scale
            vb[hd] = vo_s[r0:r0 + L, hd * dh:(hd + 1) * dh].astype(BF16)
            s_raw[hd] = lax.dot_general(qb[hd], kf[hd].astype(BF16), (((1,), (1,)), ((), ())),
                                        preferred_element_type=F32)
            m0 = m_s[hd][:, 0:1]
            c0 = c_s[hd]
            m0s[hd], c0b[hd] = m0, c0.astype(BF16)
            a_col = b_last - b_col + g[:, hd:hd + 1]
            m_new = jnp.maximum(b_last + m0, jnp.max(a_col, axis=0, keepdims=True))
            decay = jnp.exp(b_last + m0 - m_new)
            kw = (kf[hd] * jnp.exp(a_col - m_new)).astype(BF16)
            v_aug = jnp.concatenate([vb[hd], aug_ref[...]], axis=1)
            upd = lax.dot_general(kw, v_aug, (((0,), (0,)), ((), ())), preferred_element_type=F32)
            c_s[hd] = decay * c0 + upd
            m_s[hd] = jnp.broadcast_to(m_new, (1, LANES))
        yield
        w_d, inter_w, m_t = {}, {}, {}
        for hd in heads:
            b_col, b_row, ig_row = cols[hd]
            beta = ig_row - b_row
            alpha = -jnp.maximum(jnp.max(jnp.where(causal, beta, -jnp.inf), axis=-1, keepdims=True), m0s[hd])
            m_t[hd] = b_col - alpha
            w_d[hd] = jnp.exp(jnp.where(causal, alpha + beta, -jnp.inf))
            inter_w[hd] = jnp.exp(alpha + m0s[hd])
        yield
        num, den = {}, {}
        for hd in heads:
            s = s_raw[hd] * w_d[hd]
            qc = _dot(qb[hd], c0b[hd])
            num[hd] = inter_w[hd] * qc[:, 0:dh] + _dot(s.astype(BF16), vb[hd])
            den[hd] = inter_w[hd] * qc[:, dh:dh + 1] + jnp.sum(s, axis=-1, keepdims=True)
        yield
        for hd in heads:
            og = vo_s[r0:r0 + L, ML_WIDTH + hd * dh:ML_WIDTH + (hd + 1) * dh]
            hr = num[hd] * (1.0 / jnp.maximum(jnp.abs(den[hd]), jnp.exp(-m_t[hd])))
            hc = jax.nn.sigmoid(og) * hr
            hn = hc * lax.rsqrt(jnp.mean(hc * hc, axis=-1, keepdims=True) + EPS)
            hn = hn * ng_ref[:, hd * dh:(hd + 1) * dh]
            mix_s[r0:r0 + L, GM_WIDTH + hd * dh:GM_WIDTH + (hd + 1) * dh] = hn.astype(BF16)
        yield

    def stages(a, b):
        ws_masked = [jnp.where(causal2, ws_ref[j], 0.0).astype(BF16) for j in range(GM_HEADS // 2)]
        norm, *proj = project_pieces(slots[a])
        outp = out_project_pieces(mixes[a])
        pieces = []
        for i, piece in enumerate(proj):
            pieces.append(piece)
            if (i + 1) % (len(proj) // len(outp)) == 0 and (i + 1) // (len(proj) // len(outp)) <= len(outp):
                pieces.append(outp[(i + 1) // (len(proj) // len(outp)) - 1])
        groups = [pieces[i:i + MIX_PIECE_GROUP] for i in range(0, len(pieces), MIX_PIECE_GROUP)]
        sections = [gen for c in range(tm // L)
                    for gen in [mix_chunk(c, slots[b], mixes[b], ws_masked)] * MIX_SECTIONS]
        norm()
        for si, gen in enumerate(sections):
            for j, group in enumerate(groups):
                if -(-j * len(sections) // len(groups)) == si:
                    for piece in group:
                        piece()
            next(gen)
        assert all(next(gen, None) is None for gen in sections), "MIX_SECTIONS does not match mix_chunk's sections"
        slots[a][1][:, 0:TAIL, :] = slots[b][1][:, tm:tm + TAIL, :]

    parity = lax.rem(step, 2)
    pl.when(parity == 0)(lambda: stages(0, 1))
    pl.when(parity == 1)(lambda: stages(1, 0))


def _mix_layer(x, layer, g, win, wgate, gbias, lng, lnb, ws, bs, cw, cb, ng, wout, tri, aug, *, tm):
    B, S, D = x.shape
    n_in = (2 * GM_WIDTH + 4 * ML_WIDTH) // MXU_N
    n_out = wout.shape[2] // MXU_N
    tiles_per_seq = S // tm
    n_tiles = B * tiles_per_seq
    lspec = lambda a: _layer_spec(layer, a.shape[1:])

    def tile_spec(lag):
        def index(s):
            tile = jnp.clip(s - lag, 0, n_tiles - 1)
            return (tile // tiles_per_seq, tile % tiles_per_seq, 0)
        return pl.BlockSpec((1, tm, D), index)

    slot = [
        pltpu.VMEM((tm, 2 * GM_WIDTH), F32),
        pltpu.VMEM((2 * ML_WIDTH // LANES, tm + TAIL, LANES), F32),
        pltpu.VMEM((tm, 2 * ML_WIDTH), F32),
        pltpu.VMEM((tm, LANES), F32),
    ]
    return pl.pallas_call(
        functools.partial(_mix_kernel, tm=tm, n_in=n_in, n_out=n_out, tiles_per_seq=tiles_per_seq),
        grid=(n_tiles + 2,),
        in_specs=[tile_spec(0), tile_spec(2), lspec(g)]
        + _column_block_specs(layer, D, n_in)
        + [lspec(a) for a in (wgate, gbias, lng, lnb, ws, bs, cw, cb, ng)]
        + _column_block_specs(layer, wout.shape[1], n_out)
        + [_const_spec(tri.shape), _const_spec(aug.shape)],
        out_specs=tile_spec(2),
        out_shape=jax.ShapeDtypeStruct(x.shape, F32),
        scratch_shapes=slot + slot + [
            pltpu.VMEM((tm, GM_WIDTH + ML_WIDTH), BF16),
            pltpu.VMEM((tm, GM_WIDTH + ML_WIDTH), BF16),
            pltpu.VMEM((ML_HEADS, ML_HEAD_DIM, 2 * ML_HEAD_DIM), F32),
            pltpu.VMEM((ML_HEADS, 1, LANES), F32),
            pltpu.VMEM((tm, D), BF16),
        ],
        compiler_params=pltpu.CompilerParams(
            dimension_semantics=("arbitrary",), vmem_limit_bytes=VMEM_LIMIT_BYTES),
        name="token_mix",
    )(x, x, g, *([win] * n_in), wgate, gbias, lng, lnb, ws, bs, cw, cb, ng, *([wout] * n_out), tri, aug)


def _ffn_kernel(*refs, tm, final, n_up, n_down):
    x_ref, p_ref, gffn_ref = refs[0:3]
    wup_refs = refs[3:3 + n_up]
    cw_ref, cb_ref = refs[3 + n_up:5 + n_up]
    wdown_refs = refs[5 + n_up:5 + n_up + n_down]
    gple_ref = refs[5 + n_up + n_down]
    wpg_refs = refs[6 + n_up + n_down:6 + n_up + 2 * n_down]
    wple_refs = refs[6 + n_up + 2 * n_down:6 + n_up + 3 * n_down]
    gfin_ref, o_ref = refs[6 + n_up + 3 * n_down:8 + n_up + 3 * n_down]
    ubuf_s, act_s, x1_s, h2_s, pe_s = refs[8 + n_up + 3 * n_down:]
    nblk = n_up // 2
    slabs = MXU_N // LANES

    @pl.when(pl.program_id(1) == 0)
    def _():
        ubuf_s[:, :, 0:TAIL, :] = jnp.zeros((ubuf_s.shape[0], ubuf_s.shape[1], TAIL, LANES), F32)

    pb = p_ref[0].astype(BF16)
    for n in range(n_down):
        cols = slice(n * MXU_N, (n + 1) * MXU_N)
        pe_s[:, cols] = _dot(pb, wple_refs[n][...])
    h = _rms_norm(x_ref[0], gffn_ref[...]).astype(BF16)
    for j in range(nblk):
        halves = []
        for part in range(2):
            idx = j + part * nblk
            up = _dot(h, wup_refs[idx][...])
            cv = []
            for sl in range(slabs):
                cols = slice(idx * MXU_N + sl * LANES, idx * MXU_N + (sl + 1) * LANES)
                ubuf_s[idx, sl, TAIL:TAIL + tm, :] = up[:, sl * LANES:(sl + 1) * LANES]
                acc = cb_ref[:, cols]
                for k in range(FFN_CONV):
                    off = TAIL - (FFN_CONV - 1) + k
                    acc = acc + cw_ref[k:k + 1, cols] * ubuf_s[idx, sl, off:off + tm, :]
                ubuf_s[idx, sl, 0:TAIL, :] = ubuf_s[idx, sl, tm:tm + TAIL, :]
                cv.append(acc)
            halves.append(jnp.concatenate(cv, axis=1))
        a, bv = halves
        act_s[:, j * MXU_N:(j + 1) * MXU_N] = (a * jax.nn.sigmoid(a) * bv).astype(BF16)

    for n in range(n_down):
        cols = slice(n * MXU_N, (n + 1) * MXU_N)
        x1_s[:, cols] = x_ref[0, :, cols] + _dot(act_s[...], wdown_refs[n][...])

    h2_s[...] = _rms_norm(x1_s[...], gple_ref[...]).astype(BF16)
    for n in range(n_down):
        cols = slice(n * MXU_N, (n + 1) * MXU_N)
        gate = jax.nn.sigmoid(_dot(h2_s[...], wpg_refs[n][...]))
        o_ref[0, :, cols] = x1_s[:, cols] + gate * pe_s[:, cols]
    if final:
        o_ref[0] = _rms_norm(o_ref[0], gfin_ref[...])


def _ffn_layer(x, p, layer, gffn, wup, cw, cb, wdown, gple, wpg, wple, gfin, *, tm, final):
    B, S, D = x.shape
    d_ff = wdown.shape[1]
    n_up = 2 * d_ff // MXU_N
    n_down = D // MXU_N
    lspec = lambda a: _layer_spec(layer, a.shape[1:])
    return pl.pallas_call(
        functools.partial(_ffn_kernel, tm=tm, final=final, n_up=n_up, n_down=n_down),
        grid=(B, S // tm),
        in_specs=[pl.BlockSpec((1, tm, D), lambda b, t: (b, t, 0)),
                  pl.BlockSpec((None, 1, tm, p.shape[-1]), lambda b, t: (layer, b, t, 0)),
                  lspec(gffn)]
        + _column_block_specs(layer, D, n_up)
        + [lspec(cw), lspec(cb)]
        + _column_block_specs(layer, d_ff, n_down)
        + [lspec(gple)]
        + _column_block_specs(layer, D, n_down)
        + _column_block_specs(layer, p.shape[-1], n_down)
        + [_const_spec(gfin.shape)],
        out_specs=pl.BlockSpec((1, tm, D), lambda b, t: (b, t, 0)),
        out_shape=jax.ShapeDtypeStruct(x.shape, F32),
        scratch_shapes=[
            pltpu.VMEM((n_up, MXU_N // LANES, tm + TAIL, LANES), F32),
            pltpu.VMEM((tm, d_ff), BF16),
            pltpu.VMEM((tm, D), F32),
            pltpu.VMEM((tm, D), BF16),
            pltpu.VMEM((tm, D), F32),
        ],
        compiler_params=pltpu.CompilerParams(
            dimension_semantics=("arbitrary", "arbitrary"), vmem_limit_bytes=VMEM_LIMIT_BYTES),
        name="channel_mix",
    )(x, p, gffn, *([wup] * n_up), cw, cb, *([wdown] * n_down), gple, *([wpg] * n_down),
      *([wple] * n_down), gfin)


def kernel(x, p, g_mix, w_in, gm_ln_g, gm_ln_b, gm_ws, gm_bs, ml_conv_w, ml_conv_b, ml_b_i, ml_b_f,
           ml_norm_g, w_out, g_ffn, w_up, ffn_conv_w, ffn_conv_b, w_down, g_ple, w_ple_gate, w_ple,
           g_final):
    depth = g_mix.shape[0]
    seq = x.shape[1]
    tm_mix, tm_ffn = min(MIX_TILE, seq), min(FFN_TILE, seq)
    n_main = 2 * GM_WIDTH + 4 * ML_WIDTH
    n_gates = 2 * ML_HEADS

    rows = lambda a: a[:, None, :]
    win = w_in.astype(BF16)
    wgate = jnp.pad(w_in[:, :, n_main:], ((0, 0), (0, 0), (0, LANES - n_gates))).astype(BF16)
    gbias = rows(jnp.pad(jnp.concatenate([ml_b_i, ml_b_f], axis=-1), ((0, 0), (0, LANES - n_gates))))
    ws = gm_ws.reshape(depth, GM_HEADS // 2, 2, CHUNK, CHUNK).transpose(0, 1, 3, 2, 4)
    ws = ws.reshape(depth, GM_HEADS // 2, CHUNK, 2 * CHUNK)
    bs = jnp.repeat(jnp.swapaxes(gm_bs, 1, 2), GM_HEAD_DIM, axis=2)
    tri = jnp.tri(CHUNK, dtype=BF16)
    aug = jnp.zeros((CHUNK, ML_HEAD_DIM), BF16).at[:, 0].set(1)
    wout, wup, wdown = w_out.astype(BF16), w_up.astype(BF16), w_down.astype(BF16)
    wpg, wple = w_ple_gate.astype(BF16), w_ple.astype(BF16)

    for i in range(depth):
        x = _mix_layer(x, i, rows(g_mix), win, wgate, gbias, rows(gm_ln_g), rows(gm_ln_b), ws, bs,
                       ml_conv_w, rows(ml_conv_b), rows(ml_norm_g), wout, tri, aug, tm=tm_mix)
        x = _ffn_layer(x, p, i, rows(g_ffn), wup, ffn_conv_w, rows(ffn_conv_b), wdown, rows(g_ple), wpg,
                       wple, g_final.reshape(1, -1), tm=tm_ffn, final=(i == depth - 1))
    return x
```

```python
import functools

import jax
import jax.numpy as jnp
from jax import lax
from jax.experimental import pallas as pl
from jax.experimental.pallas import tpu as pltpu

EPS = 1e-6
CHUNK = 128
GM_HEADS, GM_HEAD_DIM = 8, 64
GM_WIDTH = GM_HEADS * GM_HEAD_DIM
ML_HEADS, ML_HEAD_DIM = 4, 128
ML_WIDTH = ML_HEADS * ML_HEAD_DIM
QK_CONV, FFN_CONV = 4, 3
TAIL = 8
LANES = 128
MXU_N = 256
MIX_TILE = 512
MIX_SECTIONS = 6
MIX_PIECE_GROUP = 3
FFN_TILE = 512
VMEM_LIMIT_BYTES = 56 * 1024 * 1024

F32 = jnp.float32
BF16 = jnp.bfloat16


def _dot(a, b):
    return jnp.dot(a, b, preferred_element_type=F32)


def _rms_norm(x, g):
    return x * lax.rsqrt(jnp.mean(x * x, axis=-1, keepdims=True) + EPS) * g


def _gelu(x):
    return 0.5 * x * (1.0 + lax.erf(x * (2.0 ** -0.5)))


def _split3_bf16(x):
    hi = x.astype(BF16)
    r = x - hi.astype(F32)
    mid = r.astype(BF16)
    lo = (r - mid.astype(F32)).astype(BF16)
    return hi, mid, lo


def _layer_spec(layer, shape):
    zeros = (0,) * len(shape)
    return pl.BlockSpec((None, *shape), lambda *_: (layer, *zeros), pipeline_mode=pl.Buffered(1))


def _column_block_specs(layer, rows, n_blocks, first=0):
    return [pl.BlockSpec((None, rows, MXU_N), lambda *_, n=first + n: (layer, 0, n),
                         pipeline_mode=pl.Buffered(1)) for n in range(n_blocks)]


def _const_spec(shape):
    return pl.BlockSpec(shape, lambda *_: (0,) * len(shape), pipeline_mode=pl.Buffered(1))


def _mix_kernel(*refs, tm, n_in, n_out, tiles_per_seq):
    x_ref, xlag_ref, g_ref = refs[0:3]
    win_refs = refs[3:3 + n_in]
    (wgate_ref, gbias_ref, lng_ref, lnb_ref, ws_ref, bs_ref, cw_ref, cb_ref,
     ng_ref) = refs[3 + n_in:12 + n_in]
    wout_refs = refs[12 + n_in:12 + n_in + n_out]
    tri_ref, aug_ref, o_ref = refs[12 + n_in + n_out:15 + n_in + n_out]
    scratch = refs[15 + n_in + n_out:]
    slots = (scratch[0:4], scratch[4:8])
    mixes = scratch[8:10]
    c_s, m_s, h_s = scratch[10:13]
    L, dh = CHUNK, ML_HEAD_DIM
    per = n_in // 3
    step = pl.program_id(0)

    @pl.when(step == 0)
    def _():
        for buf in (*slots[1], mixes[0], c_s, m_s):
            buf[...] = jnp.zeros(buf.shape, buf.dtype)

    @pl.when((step >= 1) & (lax.rem(step - 1, tiles_per_seq) == 0))
    def _():
        for uv_s, qk_s, vo_s, gates_s in slots:
            qk_s[:, 0:TAIL, :] = jnp.zeros((qk_s.shape[0], TAIL, LANES), F32)
        c_s[...] = jnp.zeros(c_s.shape, F32)
        m_s[...] = jnp.zeros(m_s.shape, F32)

    def project_pieces(dst):
        uv_s, qk_s, vo_s, gates_s = dst

        def norm():
            h_s[...] = _rms_norm(x_ref[0], g_ref[...]).astype(BF16)

        def block(i, n):
            def run():
                res = _dot(h_s[...], win_refs[i * per + n][...])
                cols = slice(n * MXU_N, (n + 1) * MXU_N)
                if i == 0:
                    uv_s[:, cols] = res
                elif i == 2:
                    vo_s[:, cols] = res
                else:
                    for half in range(MXU_N // LANES):
                        qk_s[n * (MXU_N // LANES) + half, TAIL:TAIL + tm, :] = res[:, half * LANES:(half + 1) * LANES]
            return run

        def gates():
            gates_s[...] = _dot(h_s[...], wgate_ref[...]) + gbias_ref[...]

        return [norm] + [block(i, n) for i in range(3) for n in range(per)] + [gates]

    def out_project_pieces(mix_s):
        def block(n):
            def run():
                cols = slice(n * MXU_N, (n + 1) * MXU_N)
                o_ref[0, :, cols] = xlag_ref[0, :, cols] + _dot(mix_s[...], wout_refs[n][...])
            return run

        return [block(n) for n in range(n_out)]

    row = lax.broadcasted_iota(jnp.int32, (L, L), 0)
    col = lax.broadcasted_iota(jnp.int32, (L, L), 1)
    causal = col <= row
    lo_half = lax.broadcasted_iota(jnp.int32, (L, LANES), 1) < GM_HEAD_DIM
    causal2 = jnp.concatenate([causal, causal], axis=1)
    k_scale = dh ** -0.5

    def mix_chunk(c, src, mix_s, ws_masked):
        uv_s, qk_s, vo_s, gates_s = src
        r0 = c * L
        u = _gelu(uv_s[r0:r0 + L, 0:GM_WIDTH])
        v = _gelu(uv_s[r0:r0 + L, GM_WIDTH:2 * GM_WIDTH])
        mu = jnp.mean(v, axis=-1, keepdims=True)
        vc = v - mu
        v = vc * lax.rsqrt(jnp.mean(vc * vc, axis=-1, keepdims=True) + EPS)
        v = v * lng_ref[...] + lnb_ref[...]
        for j in range(GM_HEADS // 2):
            vp = v[:, j * LANES:(j + 1) * LANES]
            rhs = jnp.concatenate([jnp.where(lo_half, vp, 0.0), jnp.where(lo_half, 0.0, vp)], axis=0)
            mixed = _dot(ws_masked[j], rhs.astype(BF16)) + bs_ref[:, j * LANES:(j + 1) * LANES]
            mix_s[r0:r0 + L, j * LANES:(j + 1) * LANES] = (u[:, j * LANES:(j + 1) * LANES] * mixed).astype(BF16)
        yield

        def conv_silu(slab):
            cols = slice(slab * LANES, (slab + 1) * LANES)
            acc = cb_ref[:, cols]
            for j in range(QK_CONV):
                off = r0 + TAIL - (QK_CONV - 1) + j
                acc = acc + cw_ref[j:j + 1, cols] * qk_s[slab, off:off + L, :]
            return acc * jax.nn.sigmoid(acc)

        g = gates_s[r0:r0 + L, :]
        lf = jax.nn.log_sigmoid(g)
        hi, mid, lo = _split3_bf16(lf)
        tri = tri_ref[...]
        bc = _dot(tri, hi) + _dot(tri, mid) + _dot(tri, lo)
        g_t = g.T
        bc_t = bc.T

        heads = range(ML_HEADS)
        yield
        cols, qb, kf, vb, c0b, m0s, s_raw = {}, {}, {}, {}, {}, {}, {}
        for hd in heads:
            b_col = bc[:, ML_HEADS + hd:ML_HEADS + hd + 1]
            b_row = bc_t[ML_HEADS + hd:ML_HEADS + hd + 1, :]
            b_last = b_col[L - 1:L, :]
            cols[hd] = (b_col, b_row, g_t[hd:hd + 1, :])
            qb[hd] = conv_silu(hd).astype(BF16)
            kf[hd] = conv_silu(ML_HEADS + hd) * k_scale
            vb[hd] = vo_s[r0:r0 + L, hd * dh:(hd + 1) * dh].astype(BF16)
            s_raw[hd] = lax.dot_general(qb[hd], kf[hd].astype(BF16), (((1,), (1,)), ((), ())),
                                        preferred_element_type=F32)
            m0 = m_s[hd][:, 0:1]
            c0 = c_s[hd]
            m0s[hd], c0b[hd] = m0, c0.astype(BF16)
            a_col = b_last - b_col + g[:, hd:hd + 1]
            m_new = jnp.maximum(b_last + m0, jnp.max(a_col, axis=0, keepdims=True))
            decay = jnp.exp(b_last + m0 - m_new)
            kw = (kf[hd] * jnp.exp(a_col - m_new)).astype(BF16)
            v_aug = jnp.concatenate([vb[hd], aug_ref[...]], axis=1)
            upd = lax.dot_general(kw, v_aug, (((0,), (0,)), ((), ())), preferred_element_type=F32)
            c_s[hd] = decay * c0 + upd
            m_s[hd] = jnp.broadcast_to(m_new, (1, LANES))
        yield
        w_d, inter_w, m_t = {}, {}, {}
        for hd in heads:
            b_col, b_row, ig_row = cols[hd]
            beta = ig_row - b_row
            alpha = -jnp.maximum(jnp.max(jnp.where(causal, beta, -jnp.inf), axis=-1, keepdims=True), m0s[hd])
            m_t[hd] = b_col - alpha
            w_d[hd] = jnp.exp(jnp.where(causal, alpha + beta, -jnp.inf))
            inter_w[hd] = jnp.exp(alpha + m0s[hd])
        yield
        num, den = {}, {}
        for hd in heads:
            s = s_raw[hd] * w_d[hd]
            qc = _dot(qb[hd], c0b[hd])
            num[hd] = inter_w[hd] * qc[:, 0:dh] + _dot(s.astype(BF16), vb[hd])
            den[hd] = inter_w[hd] * qc[:, dh:dh + 1] + jnp.sum(s, axis=-1, keepdims=True)
        yield
        for hd in heads:
            og = vo_s[r0:r0 + L, ML_WIDTH + hd * dh:ML_WIDTH + (hd + 1) * dh]
            hr = num[hd] * (1.0 / jnp.maximum(jnp.abs(den[hd]), jnp.exp(-m_t[hd])))
            hc = jax.nn.sigmoid(og) * hr
            hn = hc * lax.rsqrt(jnp.mean(hc * hc, axis=-1, keepdims=True) + EPS)
            hn = hn * ng_ref[:, hd * dh:(hd + 1) * dh]
            mix_s[r0:r0 + L, GM_WIDTH + hd * dh:GM_WIDTH + (hd + 1) * dh] = hn.astype(BF16)
        yield

    def stages(a, b):
        ws_masked = [jnp.where(causal2, ws_ref[j], 0.0).astype(BF16) for j in range(GM_HEADS // 2)]
        norm, *pieces = project_pieces(slots[a]) + out_project_pieces(mixes[a])
        groups = [pieces[i:i + MIX_PIECE_GROUP] for i in range(0, len(pieces), MIX_PIECE_GROUP)]
        sections = [gen for c in range(tm // L)
                    for gen in [mix_chunk(c, slots[b], mixes[b], ws_masked)] * MIX_SECTIONS]
        norm()
        for si, gen in enumerate(sections):
            for j, group in enumerate(groups):
                if -(-j * len(sections) // len(groups)) == si:
                    for piece in group:
                        piece()
            next(gen)
        assert all(next(gen, None) is None for gen in sections), "MIX_SECTIONS does not match mix_chunk's sections"
        slots[a][1][:, 0:TAIL, :] = slots[b][1][:, tm:tm + TAIL, :]

    parity = lax.rem(step, 2)
    pl.when(parity == 0)(lambda: stages(0, 1))
    pl.when(parity == 1)(lambda: stages(1, 0))


def _mix_layer(x, layer, g, win, wgate, gbias, lng, lnb, ws, bs, cw, cb, ng, wout, tri, aug, *, tm):
    B, S, D = x.shape
    n_in = (2 * GM_WIDTH + 4 * ML_WIDTH) // MXU_N
    n_out = wout.shape[2] // MXU_N
    tiles_per_seq = S // tm
    n_tiles = B * tiles_per_seq
    lspec = lambda a: _layer_spec(layer, a.shape[1:])

    def tile_spec(lag):
        def index(s):
            tile = jnp.clip(s - lag, 0, n_tiles - 1)
            return (tile // tiles_per_seq, tile % tiles_per_seq, 0)
        return pl.BlockSpec((1, tm, D), index)

    slot = [
        pltpu.VMEM((tm, 2 * GM_WIDTH), F32),
        pltpu.VMEM((2 * ML_WIDTH // LANES, tm + TAIL, LANES), F32),
        pltpu.VMEM((tm, 2 * ML_WIDTH), F32),
        pltpu.VMEM((tm, LANES), F32),
    ]
    return pl.pallas_call(
        functools.partial(_mix_kernel, tm=tm, n_in=n_in, n_out=n_out, tiles_per_seq=tiles_per_seq),
        grid=(n_tiles + 2,),
        in_specs=[tile_spec(0), tile_spec(2), lspec(g)]
        + _column_block_specs(layer, D, n_in)
        + [lspec(a) for a in (wgate, gbias, lng, lnb, ws, bs, cw, cb, ng)]
        + _column_block_specs(layer, wout.shape[1], n_out)
        + [_const_spec(tri.shape), _const_spec(aug.shape)],
        out_specs=tile_spec(2),
        out_shape=jax.ShapeDtypeStruct(x.shape, F32),
        scratch_shapes=slot + slot + [
            pltpu.VMEM((tm, GM_WIDTH + ML_WIDTH), BF16),
            pltpu.VMEM((tm, GM_WIDTH + ML_WIDTH), BF16),
            pltpu.VMEM((ML_HEADS, ML_HEAD_DIM, 2 * ML_HEAD_DIM), F32),
            pltpu.VMEM((ML_HEADS, 1, LANES), F32),
            pltpu.VMEM((tm, D), BF16),
        ],
        compiler_params=pltpu.CompilerParams(
            dimension_semantics=("arbitrary",), vmem_limit_bytes=VMEM_LIMIT_BYTES),
        name="token_mix",
    )(x, x, g, *([win] * n_in), wgate, gbias, lng, lnb, ws, bs, cw, cb, ng, *([wout] * n_out), tri, aug)


def _ffn_kernel(*refs, tm, final, n_up, n_down, tiles_per_seq):
    x_ref, p_ref, gffn_ref = refs[0:3]
    wup_refs = refs[3:3 + n_up]
    cw_ref, cb_ref = refs[3 + n_up:5 + n_up]
    wdown_refs = refs[5 + n_up:5 + n_up + n_down]
    gple_ref = refs[5 + n_up + n_down]
    wpg_refs = refs[6 + n_up + n_down:6 + n_up + 2 * n_down]
    wple_refs = refs[6 + n_up + 2 * n_down:6 + n_up + 3 * n_down]
    gfin_ref, o_ref = refs[6 + n_up + 3 * n_down:8 + n_up + 3 * n_down]
    ubuf_s, act_s, x1_s, h2_s, pe_s = refs[8 + n_up + 3 * n_down:]
    nblk = n_up // 2
    slabs = MXU_N // LANES
    step = pl.program_id(0)

    @pl.when(step == 0)
    def _():
        x1_s[...] = jnp.zeros(x1_s.shape, F32)

    @pl.when(lax.rem(step, tiles_per_seq) == 0)
    def _():
        ubuf_s[:, :, 0:TAIL, :] = jnp.zeros((ubuf_s.shape[0], ubuf_s.shape[1], TAIL, LANES), F32)

    pb = p_ref[0].astype(BF16)
    for n in range(n_down):
        cols = slice(n * MXU_N, (n + 1) * MXU_N)
        pe_s[:, cols] = _dot(pb, wple_refs[n][...])
    h = _rms_norm(x_ref[0], gffn_ref[...]).astype(BF16)
    h2_s[...] = _rms_norm(x1_s[...], gple_ref[...]).astype(BF16)

    def gate_block(n):
        cols = slice(n * MXU_N, (n + 1) * MXU_N)
        gate = jax.nn.sigmoid(_dot(h2_s[...], wpg_refs[n][...]))
        o_ref[0, :, cols] = x1_s[:, cols] + gate * pe_s[:, cols]

    gate_after = {((n + 1) * nblk) // (n_down + 1): n for n in range(n_down)}
    for j in range(nblk):
        halves = []
        for part in range(2):
            idx = j + part * nblk
            up = _dot(h, wup_refs[idx][...])
            cv = []
            for sl in range(slabs):
                cols = slice(idx * MXU_N + sl * LANES, idx * MXU_N + (sl + 1) * LANES)
                ubuf_s[idx, sl, TAIL:TAIL + tm, :] = up[:, sl * LANES:(sl + 1) * LANES]
                acc = cb_ref[:, cols]
                for k in range(FFN_CONV):
                    off = TAIL - (FFN_CONV - 1) + k
                    acc = acc + cw_ref[k:k + 1, cols] * ubuf_s[idx, sl, off:off + tm, :]
                ubuf_s[idx, sl, 0:TAIL, :] = ubuf_s[idx, sl, tm:tm + TAIL, :]
                cv.append(acc)
            halves.append(jnp.concatenate(cv, axis=1))
        a, bv = halves
        act_s[:, j * MXU_N:(j + 1) * MXU_N] = (a * jax.nn.sigmoid(a) * bv).astype(BF16)
        if j in gate_after:
            gate_block(gate_after[j])
    if final:
        o_ref[0] = _rms_norm(o_ref[0], gfin_ref[...])

    for n in range(n_down):
        cols = slice(n * MXU_N, (n + 1) * MXU_N)
        x1_s[:, cols] = x_ref[0, :, cols] + _dot(act_s[...], wdown_refs[n][...])


def _ffn_layer(x, p, layer, gffn, wup, cw, cb, wdown, gple, wpg, wple, gfin, *, tm, final):
    B, S, D = x.shape
    d_ff = wdown.shape[1]
    n_up = 2 * d_ff // MXU_N
    n_down = D // MXU_N
    tiles_per_seq = S // tm
    n_tiles = B * tiles_per_seq
    lspec = lambda a: _layer_spec(layer, a.shape[1:])

    def tile_index(lag):
        def index(s):
            tile = jnp.clip(s - lag, 0, n_tiles - 1)
            return (tile // tiles_per_seq, tile % tiles_per_seq, 0)
        return index

    return pl.pallas_call(
        functools.partial(_ffn_kernel, tm=tm, final=final, n_up=n_up, n_down=n_down, tiles_per_seq=tiles_per_seq),
        grid=(n_tiles + 1,),
        in_specs=[pl.BlockSpec((1, tm, D), tile_index(0)),
                  pl.BlockSpec((None, 1, tm, p.shape[-1]), lambda s: (layer, *tile_index(1)(s))),
                  lspec(gffn)]
        + _column_block_specs(layer, D, n_up)
        + [lspec(cw), lspec(cb)]
        + _column_block_specs(layer, d_ff, n_down)
        + [lspec(gple)]
        + _column_block_specs(layer, D, n_down)
        + _column_block_specs(layer, p.shape[-1], n_down)
        + [_const_spec(gfin.shape)],
        out_specs=pl.BlockSpec((1, tm, D), tile_index(1)),
        out_shape=jax.ShapeDtypeStruct(x.shape, F32),
        scratch_shapes=[
            pltpu.VMEM((n_up, MXU_N // LANES, tm + TAIL, LANES), F32),
            pltpu.VMEM((tm, d_ff), BF16),
            pltpu.VMEM((tm, D), F32),
            pltpu.VMEM((tm, D), BF16),
            pltpu.VMEM((tm, D), F32),
        ],
        compiler_params=pltpu.CompilerParams(
            dimension_semantics=("arbitrary",), vmem_limit_bytes=VMEM_LIMIT_BYTES),
        name="channel_mix",
    )(x, p, gffn, *([wup] * n_up), cw, cb, *([wdown] * n_down), gple, *([wpg] * n_down),
      *([wple] * n_down), gfin)


def kernel(x, p, g_mix, w_in, gm_ln_g, gm_ln_b, gm_ws, gm_bs, ml_conv_w, ml_conv_b, ml_b_i, ml_b_f,
           ml_norm_g, w_out, g_ffn, w_up, ffn_conv_w, ffn_conv_b, w_down, g_ple, w_ple_gate, w_ple,
           g_final):
    depth = g_mix.shape[0]
    seq = x.shape[1]
    tm_mix, tm_ffn = min(MIX_TILE, seq), min(FFN_TILE, seq)
    n_main = 2 * GM_WIDTH + 4 * ML_WIDTH
    n_gates = 2 * ML_HEADS

    rows = lambda a: a[:, None, :]
    win = w_in.astype(BF16)
    wgate = jnp.pad(w_in[:, :, n_main:], ((0, 0), (0, 0), (0, LANES - n_gates))).astype(BF16)
    gbias = rows(jnp.pad(jnp.concatenate([ml_b_i, ml_b_f], axis=-1), ((0, 0), (0, LANES - n_gates))))
    ws = gm_ws.reshape(depth, GM_HEADS // 2, 2, CHUNK, CHUNK).transpose(0, 1, 3, 2, 4)
    ws = ws.reshape(depth, GM_HEADS // 2, CHUNK, 2 * CHUNK)
    bs = jnp.repeat(jnp.swapaxes(gm_bs, 1, 2), GM_HEAD_DIM, axis=2)
    tri = jnp.tri(CHUNK, dtype=BF16)
    aug = jnp.zeros((CHUNK, ML_HEAD_DIM), BF16).at[:, 0].set(1)
    wout, wup, wdown = w_out.astype(BF16), w_up.astype(BF16), w_down.astype(BF16)
    wpg, wple = w_ple_gate.astype(BF16), w_ple.astype(BF16)

    for i in range(depth):
        x = _mix_layer(x, i, rows(g_mix), win, wgate, gbias, rows(gm_ln_g), rows(gm_ln_b), ws, bs,
                       ml_conv_w, rows(ml_conv_b), rows(ml_norm_g), wout, tri, aug, tm=tm_mix)
        x = _ffn_layer(x, p, i, rows(g_ffn), wup, ffn_conv_w, rows(ffn_conv_b), wdown, rows(g_ple), wpg,
                       wple, g_final.reshape(1, -1), tm=tm_ffn, final=(i == depth - 1))
    return x
```

```python
import functools

import jax
import jax.numpy as jnp
from jax import lax
from jax.experimental import pallas as pl
from jax.experimental.pallas import tpu as pltpu

EPS = 1e-6
CHUNK = 128
GM_HEADS, GM_HEAD_DIM = 8, 64
GM_WIDTH = GM_HEADS * GM_HEAD_DIM
ML_HEADS, ML_HEAD_DIM = 4, 128
ML_WIDTH = ML_HEADS * ML_HEAD_DIM
QK_CONV, FFN_CONV = 4, 3
TAIL = 8
LANES = 128
MXU_N = 256
MIX_TILE = 256
MIX_SECTIONS = 6
MIX_PIECE_GROUP = 2
FFN_TILE = 512
VMEM_LIMIT_BYTES = 56 * 1024 * 1024

F32 = jnp.float32
BF16 = jnp.bfloat16


def _dot(a, b):
    return jnp.dot(a, b, preferred_element_type=F32)


def _rms_norm(x, g):
    return x * lax.rsqrt(jnp.mean(x * x, axis=-1, keepdims=True) + EPS) * g


def _gelu(x):
    return 0.5 * x * (1.0 + lax.erf(x * (2.0 ** -0.5)))


def _split3_bf16(x):
    hi = x.astype(BF16)
    r = x - hi.astype(F32)
    mid = r.astype(BF16)
    lo = (r - mid.astype(F32)).astype(BF16)
    return hi, mid, lo


def _layer_spec(layer, shape):
    zeros = (0,) * len(shape)
    return pl.BlockSpec((None, *shape), lambda *_: (layer, *zeros), pipeline_mode=pl.Buffered(1))


def _column_block_specs(layer, rows, n_blocks, first=0):
    return [pl.BlockSpec((None, rows, MXU_N), lambda *_, n=first + n: (layer, 0, n),
                         pipeline_mode=pl.Buffered(1)) for n in range(n_blocks)]


def _const_spec(shape):
    return pl.BlockSpec(shape, lambda *_: (0,) * len(shape), pipeline_mode=pl.Buffered(1))


def _mix_kernel(*refs, tm, n_in, n_out, tiles_per_seq):
    x_ref, xlag_ref, g_ref = refs[0:3]
    win_refs = refs[3:3 + n_in]
    (wgate_ref, gbias_ref, lng_ref, lnb_ref, ws_ref, bs_ref, cw_ref, cb_ref,
     ng_ref) = refs[3 + n_in:12 + n_in]
    wout_refs = refs[12 + n_in:12 + n_in + n_out]
    tri_ref, aug_ref, o_ref = refs[12 + n_in + n_out:15 + n_in + n_out]
    scratch = refs[15 + n_in + n_out:]
    slots = (scratch[0:4], scratch[4:8])
    mixes = scratch[8:10]
    c_s, m_s, h_s = scratch[10:13]
    L, dh = CHUNK, ML_HEAD_DIM
    per = n_in // 3
    step = pl.program_id(0)

    @pl.when(step == 0)
    def _():
        for buf in (*slots[1], mixes[0], c_s, m_s):
            buf[...] = jnp.zeros(buf.shape, buf.dtype)

    @pl.when((step >= 1) & (lax.rem(step - 1, tiles_per_seq) == 0))
    def _():
        for uv_s, qk_s, vo_s, gates_s in slots:
            qk_s[:, 0:TAIL, :] = jnp.zeros((qk_s.shape[0], TAIL, LANES), F32)
        c_s[...] = jnp.zeros(c_s.shape, F32)
        m_s[...] = jnp.zeros(m_s.shape, F32)

    def project_pieces(dst):
        uv_s, qk_s, vo_s, gates_s = dst

        def norm():
            h_s[...] = _rms_norm(x_ref[0], g_ref[...]).astype(BF16)

        def block(i, n):
            def run():
                res = _dot(h_s[...], win_refs[i * per + n][...])
                cols = slice(n * MXU_N, (n + 1) * MXU_N)
                if i == 0:
                    uv_s[:, cols] = res
                elif i == 2:
                    vo_s[:, cols] = res
                else:
                    for half in range(MXU_N // LANES):
                        qk_s[n * (MXU_N // LANES) + half, TAIL:TAIL + tm, :] = res[:, half * LANES:(half + 1) * LANES]
            return run

        def gates():
            gates_s[...] = _dot(h_s[...], wgate_ref[...]) + gbias_ref[...]

        return [norm] + [block(i, n) for i in range(3) for n in range(per)] + [gates]

    def out_project_pieces(mix_s):
        def block(n):
            def run():
                cols = slice(n * MXU_N, (n + 1) * MXU_N)
                o_ref[0, :, cols] = xlag_ref[0, :, cols] + _dot(mix_s[...], wout_refs[n][...])
            return run

        return [block(n) for n in range(n_out)]

    row = lax.broadcasted_iota(jnp.int32, (L, L), 0)
    col = lax.broadcasted_iota(jnp.int32, (L, L), 1)
    causal = col <= row
    lo_half = lax.broadcasted_iota(jnp.int32, (L, LANES), 1) < GM_HEAD_DIM
    causal2 = jnp.concatenate([causal, causal], axis=1)
    k_scale = dh ** -0.5

    def mix_chunk(c, src, mix_s, ws_masked):
        uv_s, qk_s, vo_s, gates_s = src
        r0 = c * L
        u = _gelu(uv_s[r0:r0 + L, 0:GM_WIDTH])
        v = _gelu(uv_s[r0:r0 + L, GM_WIDTH:2 * GM_WIDTH])
        mu = jnp.mean(v, axis=-1, keepdims=True)
        vc = v - mu
        v = vc * lax.rsqrt(jnp.mean(vc * vc, axis=-1, keepdims=True) + EPS)
        v = v * lng_ref[...] + lnb_ref[...]
        for j in range(GM_HEADS // 2):
            vp = v[:, j * LANES:(j + 1) * LANES]
            rhs = jnp.concatenate([jnp.where(lo_half, vp, 0.0), jnp.where(lo_half, 0.0, vp)], axis=0)
            mixed = _dot(ws_masked[j], rhs.astype(BF16)) + bs_ref[:, j * LANES:(j + 1) * LANES]
            mix_s[r0:r0 + L, j * LANES:(j + 1) * LANES] = (u[:, j * LANES:(j + 1) * LANES] * mixed).astype(BF16)
        yield

        def conv_silu(slab):
            cols = slice(slab * LANES, (slab + 1) * LANES)
            acc = cb_ref[:, cols]
            for j in range(QK_CONV):
                off = r0 + TAIL - (QK_CONV - 1) + j
                acc = acc + cw_ref[j:j + 1, cols] * qk_s[slab, off:off + L, :]
            return acc * jax.nn.sigmoid(acc)

        g = gates_s[r0:r0 + L, :]
        lf = jax.nn.log_sigmoid(g)
        hi, mid, lo = _split3_bf16(lf)
        tri = tri_ref[...]
        bc = _dot(tri, hi) + _dot(tri, mid) + _dot(tri, lo)
        g_t = g.T
        bc_t = bc.T

        heads = range(ML_HEADS)
        yield
        cols, qb, kf, vb, c0b, m0s, s_raw = {}, {}, {}, {}, {}, {}, {}
        for hd in heads:
            b_col = bc[:, ML_HEADS + hd:ML_HEADS + hd + 1]
            b_row = bc_t[ML_HEADS + hd:ML_HEADS + hd + 1, :]
            b_last = b_col[L - 1:L, :]
            cols[hd] = (b_col, b_row, g_t[hd:hd + 1, :])
            qb[hd] = conv_silu(hd).astype(BF16)
            kf[hd] = conv_silu(ML_HEADS + hd) * k_scale
            vb[hd] = vo_s[r0:r0 + L, hd * dh:(hd + 1) * dh].astype(BF16)
            s_raw[hd] = lax.dot_general(qb[hd], kf[hd].astype(BF16), (((1,), (1,)), ((), ())),
                                        preferred_element_type=F32)
            m0 = m_s[hd][:, 0:1]
            c0 = c_s[hd]
            m0s[hd], c0b[hd] = m0, c0.astype(BF16)
            a_col = b_last - b_col + g[:, hd:hd + 1]
            m_new = jnp.maximum(b_last + m0, jnp.max(a_col, axis=0, keepdims=True))
            decay = jnp.exp(b_last + m0 - m_new)
            kw = (kf[hd] * jnp.exp(a_col - m_new)).astype(BF16)
            v_aug = jnp.concatenate([vb[hd], aug_ref[...]], axis=1)
            upd = lax.dot_general(kw, v_aug, (((0,), (0,)), ((), ())), preferred_element_type=F32)
            c_s[hd] = decay * c0 + upd
            m_s[hd] = jnp.broadcast_to(m_new, (1, LANES))
        yield
        w_d, inter_w, m_t = {}, {}, {}
        for hd in heads:
            b_col, b_row, ig_row = cols[hd]
            beta = ig_row - b_row
            alpha = -jnp.maximum(jnp.max(jnp.where(causal, beta, -jnp.inf), axis=-1, keepdims=True), m0s[hd])
            m_t[hd] = b_col - alpha
            w_d[hd] = jnp.exp(jnp.where(causal, alpha + beta, -jnp.inf))
            inter_w[hd] = jnp.exp(alpha + m0s[hd])
        yield
        num, den = {}, {}
        for hd in heads:
            s = s_raw[hd] * w_d[hd]
            qc = _dot(qb[hd], c0b[hd])
            num[hd] = inter_w[hd] * qc[:, 0:dh] + _dot(s.astype(BF16), vb[hd])
            den[hd] = inter_w[hd] * qc[:, dh:dh + 1] + jnp.sum(s, axis=-1, keepdims=True)
        yield
        for hd in heads:
            og = vo_s[r0:r0 + L, ML_WIDTH + hd * dh:ML_WIDTH + (hd + 1) * dh]
            hr = num[hd] * (1.0 / jnp.maximum(jnp.abs(den[hd]), jnp.exp(-m_t[hd])))
            hc = jax.nn.sigmoid(og) * hr
            hn = hc * lax.rsqrt(jnp.mean(hc * hc, axis=-1, keepdims=True) + EPS)
            hn = hn * ng_ref[:, hd * dh:(hd + 1) * dh]
            mix_s[r0:r0 + L, GM_WIDTH + hd * dh:GM_WIDTH + (hd + 1) * dh] = hn.astype(BF16)
        yield

    def stages(a, b):
        ws_masked = [jnp.where(causal2, ws_ref[j], 0.0).astype(BF16) for j in range(GM_HEADS // 2)]
        norm, *pieces = project_pieces(slots[a]) + out_project_pieces(mixes[a])
        groups = [pieces[i:i + MIX_PIECE_GROUP] for i in range(0, len(pieces), MIX_PIECE_GROUP)]
        sections = [gen for c in range(tm // L)
                    for gen in [mix_chunk(c, slots[b], mixes[b], ws_masked)] * MIX_SECTIONS]
        norm()
        for si, gen in enumerate(sections):
            for j, group in enumerate(groups):
                if -(-j * len(sections) // len(groups)) == si:
                    for piece in group:
                        piece()
            next(gen)
        assert all(next(gen, None) is None for gen in sections), "MIX_SECTIONS does not match mix_chunk's sections"
        slots[a][1][:, 0:TAIL, :] = slots[b][1][:, tm:tm + TAIL, :]

    parity = lax.rem(step, 2)
    pl.when(parity == 0)(lambda: stages(0, 1))
    pl.when(parity == 1)(lambda: stages(1, 0))


def _mix_layer(x, layer, g, win, wgate, gbias, lng, lnb, ws, bs, cw, cb, ng, wout, tri, aug, *, tm):
    B, S, D = x.shape
    n_in = (2 * GM_WIDTH + 4 * ML_WIDTH) // MXU_N
    n_out = wout.shape[2] // MXU_N
    tiles_per_seq = S // tm
    n_tiles = B * tiles_per_seq
    lspec = lambda a: _layer_spec(layer, a.shape[1:])

    def tile_spec(lag):
        def index(s):
            tile = jnp.clip(s - lag, 0, n_tiles - 1)
            return (tile // tiles_per_seq, tile % tiles_per_seq, 0)
        return pl.BlockSpec((1, tm, D), index)

    slot = [
        pltpu.VMEM((tm, 2 * GM_WIDTH), F32),
        pltpu.VMEM((2 * ML_WIDTH // LANES, tm + TAIL, LANES), F32),
        pltpu.VMEM((tm, 2 * ML_WIDTH), F32),
        pltpu.VMEM((tm, LANES), F32),
    ]
    return pl.pallas_call(
        functools.partial(_mix_kernel, tm=tm, n_in=n_in, n_out=n_out, tiles_per_seq=tiles_per_seq),
        grid=(n_tiles + 2,),
        in_specs=[tile_spec(0), tile_spec(2), lspec(g)]
        + _column_block_specs(layer, D, n_in)
        + [lspec(a) for a in (wgate, gbias, lng, lnb, ws, bs, cw, cb, ng)]
        + _column_block_specs(layer, wout.shape[1], n_out)
        + [_const_spec(tri.shape), _const_spec(aug.shape)],
        out_specs=tile_spec(2),
        out_shape=jax.ShapeDtypeStruct(x.shape, F32),
        scratch_shapes=slot + slot + [
            pltpu.VMEM((tm, GM_WIDTH + ML_WIDTH), BF16),
            pltpu.VMEM((tm, GM_WIDTH + ML_WIDTH), BF16),
            pltpu.VMEM((ML_HEADS, ML_HEAD_DIM, 2 * ML_HEAD_DIM), F32),
            pltpu.VMEM((ML_HEADS, 1, LANES), F32),
            pltpu.VMEM((tm, D), BF16),
        ],
        compiler_params=pltpu.CompilerParams(
            dimension_semantics=("arbitrary",), vmem_limit_bytes=VMEM_LIMIT_BYTES),
        name="token_mix",
    )(x, x, g, *([win] * n_in), wgate, gbias, lng, lnb, ws, bs, cw, cb, ng, *([wout] * n_out), tri, aug)


def _ffn_kernel(*refs, tm, final, n_up, n_down):
    x_ref, p_ref, gffn_ref = refs[0:3]
    wup_refs = refs[3:3 + n_up]
    cw_ref, cb_ref = refs[3 + n_up:5 + n_up]
    wdown_refs = refs[5 + n_up:5 + n_up + n_down]
    gple_ref = refs[5 + n_up + n_down]
    wpg_refs = refs[6 + n_up + n_down:6 + n_up + 2 * n_down]
    wple_refs = refs[6 + n_up + 2 * n_down:6 + n_up + 3 * n_down]
    gfin_ref, o_ref = refs[6 + n_up + 3 * n_down:8 + n_up + 3 * n_down]
    ubuf_s, act_s, x1_s, h2_s, pe_s = refs[8 + n_up + 3 * n_down:]
    nblk = n_up // 2
    slabs = MXU_N // LANES

    @pl.when(pl.program_id(1) == 0)
    def _():
        ubuf_s[:, :, 0:TAIL, :] = jnp.zeros((ubuf_s.shape[0], ubuf_s.shape[1], TAIL, LANES), F32)

    pb = p_ref[0].astype(BF16)
    for n in range(n_down):
        cols = slice(n * MXU_N, (n + 1) * MXU_N)
        pe_s[:, cols] = _dot(pb, wple_refs[n][...])
    h = _rms_norm(x_ref[0], gffn_ref[...]).astype(BF16)
    for j in range(nblk):
        halves = []
        for part in range(2):
            idx = j + part * nblk
            up = _dot(h, wup_refs[idx][...])
            cv = []
            for sl in range(slabs):
                cols = slice(idx * MXU_N + sl * LANES, idx * MXU_N + (sl + 1) * LANES)
                ubuf_s[idx, sl, TAIL:TAIL + tm, :] = up[:, sl * LANES:(sl + 1) * LANES]
                acc = cb_ref[:, cols]
                for k in range(FFN_CONV):
                    off = TAIL - (FFN_CONV - 1) + k
                    acc = acc + cw_ref[k:k + 1, cols] * ubuf_s[idx, sl, off:off + tm, :]
                ubuf_s[idx, sl, 0:TAIL, :] = ubuf_s[idx, sl, tm:tm + TAIL, :]
                cv.append(acc)
            halves.append(jnp.concatenate(cv, axis=1))
        a, bv = halves
        act_s[:, j * MXU_N:(j + 1) * MXU_N] = (a * jax.nn.sigmoid(a) * bv).astype(BF16)

    for n in range(n_down):
        cols = slice(n * MXU_N, (n + 1) * MXU_N)
        x1_s[:, cols] = x_ref[0, :, cols] + _dot(act_s[...], wdown_refs[n][...])

    h2_s[...] = _rms_norm(x1_s[...], gple_ref[...]).astype(BF16)
    for n in range(n_down):
        cols = slice(n * MXU_N, (n + 1) * MXU_N)
        gate = jax.nn.sigmoid(_dot(h2_s[...], wpg_refs[n][...]))
        o_ref[0, :, cols] = x1_s[:, cols] + gate * pe_s[:, cols]
    if final:
        o_ref[0] = _rms_norm(o_ref[0], gfin_ref[...])


def _ffn_layer(x, p, layer, gffn, wup, cw, cb, wdown, gple, wpg, wple, gfin, *, tm, final):
    B, S, D = x.shape
    d_ff = wdown.shape[1]
    n_up = 2 * d_ff // MXU_N
    n_down = D // MXU_N
    lspec = lambda a: _layer_spec(layer, a.shape[1:])
    return pl.pallas_call(
        functools.partial(_ffn_kernel, tm=tm, final=final, n_up=n_up, n_down=n_down),
        grid=(B, S // tm),
        in_specs=[pl.BlockSpec((1, tm, D), lambda b, t: (b, t, 0)),
                  pl.BlockSpec((None, 1, tm, p.shape[-1]), lambda b, t: (layer, b, t, 0)),
                  lspec(gffn)]
        + _column_block_specs(layer, D, n_up)
        + [lspec(cw), lspec(cb)]
        + _column_block_specs(layer, d_ff, n_down)
        + [lspec(gple)]
        + _column_block_specs(layer, D, n_down)
        + _column_block_specs(layer, p.shape[-1], n_down)
        + [_const_spec(gfin.shape)],
        out_specs=pl.BlockSpec((1, tm, D), lambda b, t: (b, t, 0)),
        out_shape=jax.ShapeDtypeStruct(x.shape, F32),
        scratch_shapes=[
            pltpu.VMEM((n_up, MXU_N // LANES, tm + TAIL, LANES), F32),
            pltpu.VMEM((tm, d_ff), BF16),
            pltpu.VMEM((tm, D), F32),
            pltpu.VMEM((tm, D), BF16),
            pltpu.VMEM((tm, D), F32),
        ],
        compiler_params=pltpu.CompilerParams(
            dimension_semantics=("arbitrary", "arbitrary"), vmem_limit_bytes=VMEM_LIMIT_BYTES),
        name="channel_mix",
    )(x, p, gffn, *([wup] * n_up), cw, cb, *([wdown] * n_down), gple, *([wpg] * n_down),
      *([wple] * n_down), gfin)


def kernel(x, p, g_mix, w_in, gm_ln_g, gm_ln_b, gm_ws, gm_bs, ml_conv_w, ml_conv_b, ml_b_i, ml_b_f,
           ml_norm_g, w_out, g_ffn, w_up, ffn_conv_w, ffn_conv_b, w_down, g_ple, w_ple_gate, w_ple,
           g_final):
    depth = g_mix.shape[0]
    seq = x.shape[1]
    tm_mix, tm_ffn = min(MIX_TILE, seq), min(FFN_TILE, seq)
    n_main = 2 * GM_WIDTH + 4 * ML_WIDTH
    n_gates = 2 * ML_HEADS

    rows = lambda a: a[:, None, :]
    win = w_in.astype(BF16)
    wgate = jnp.pad(w_in[:, :, n_main:], ((0, 0), (0, 0), (0, LANES - n_gates))).astype(BF16)
    gbias = rows(jnp.pad(jnp.concatenate([ml_b_i, ml_b_f], axis=-1), ((0, 0), (0, LANES - n_gates))))
    ws = gm_ws.reshape(depth, GM_HEADS // 2, 2, CHUNK, CHUNK).transpose(0, 1, 3, 2, 4)
    ws = ws.reshape(depth, GM_HEADS // 2, CHUNK, 2 * CHUNK)
    bs = jnp.repeat(jnp.swapaxes(gm_bs, 1, 2), GM_HEAD_DIM, axis=2)
    tri = jnp.tri(CHUNK, dtype=BF16)
    aug = jnp.zeros((CHUNK, ML_HEAD_DIM), BF16).at[:, 0].set(1)
    wout, wup, wdown = w_out.astype(BF16), w_up.astype(BF16), w_down.astype(BF16)
    wpg, wple = w_ple_gate.astype(BF16), w_ple.astype(BF16)

    for i in range(depth):
        x = _mix_layer(x, i, rows(g_mix), win, wgate, gbias, rows(gm_ln_g), rows(gm_ln_b), ws, bs,
                       ml_conv_w, rows(ml_conv_b), rows(ml_norm_g), wout, tri, aug, tm=tm_mix)
        x = _ffn_layer(x, p, i, rows(g_ffn), wup, ffn_conv_w, rows(ffn_conv_b), wdown, rows(g_ple), wpg,
                       wple, g_final.reshape(1, -1), tm=tm_ffn, final=(i == depth - 1))
    return x
```
